```python
import math
import jax, jax.numpy as jnp
from jax import lax
import numpy as np

D_MODEL = 2048
BATCH = 8
SEQ = 4096
DEPTH = 2

F32 = jnp.float32
EPS = 1e-6
D_MIX = D_MODEL
GDN_HEADS = 8
GDN_DK = 128
GDN_DV = 128
GDN_W = GDN_HEADS * GDN_DV
GDN_CONV = 3
GDN_CHUNK = 64
HY_GROUPS = 4
HY_W = D_MIX // 4
HY_CONV = 3
HY_EMB = 33
HY_BANDS = (HY_EMB - 1) // 2
HY_FFN = 64
HY_STEEP_DECAY_PCT = 0.3
HY_SHALLOW_DECAY_PCT = 1.5
HY_DECAY_TARGET = 1e-2
FN_GROUPS = 4
FN_W = D_MIX - GDN_W - HY_W
FN_GC = FN_W // FN_GROUPS
OFF_Z = 3 * GDN_W
OFF_BA = 4 * GDN_W
OFF_HY = OFF_BA + 4 * GDN_HEADS
OFF_FN = OFF_HY + 3 * HY_W
IN_COLS = OFF_FN + FN_W
PEER_HEADS = 8
PEER_NKEYS = 128
PEER_N = PEER_NKEYS * PEER_NKEYS
PEER_QDIM = 256
PEER_HALF = PEER_QDIM // 2
PEER_TOPK = 16
PEER_BLOCK = 128

kernel_name = 'hybrid_gdn_hyena_fnet_peer_encoder'


def rms_norm(x, w):
    xf = x.astype(F32)
    y = xf * lax.rsqrt(jnp.mean(xf * xf, axis=-1, keepdims=True) + EPS)
    return (y * w.astype(F32)).astype(x.dtype)


def depthwise_conv_centred(x, w):
    c = x.shape[-1]
    return lax.conv_general_dilated(x, w[:, None, :].astype(x.dtype), window_strides=(1,), padding='SAME',
                                    dimension_numbers=('NWC', 'WIO', 'NWC'), feature_group_count=c)


def l2_normalize(t):
    return t * lax.rsqrt(jnp.sum(t * t, axis=-1, keepdims=True) + EPS)


def gated_delta_chunked(q, k, v, g, beta):
    b, h, l, dk = q.shape
    dv = v.shape[-1]
    c = GDN_CHUNK
    n = l // c
    q = q.reshape(b, h, n, c, dk)
    k = k.reshape(b, h, n, c, dk)
    v = v.reshape(b, h, n, c, dv)
    g = jnp.cumsum(g.reshape(b, h, n, c), axis=-1)
    beta = beta.reshape(b, h, n, c)
    incl = jnp.tril(jnp.ones((c, c), dtype=bool))
    strict = jnp.tril(jnp.ones((c, c), dtype=bool), k=-1)
    diff = g[..., :, None] - g[..., None, :]
    decay = jnp.where(incl, jnp.exp(jnp.where(incl, diff, 0.0)), 0.0)
    k_beta = k * beta[..., None]
    a = jnp.where(strict, jnp.einsum('bhnid,bhnjd->bhnij', k_beta, k) * decay, 0.0)
    t_mat = a + jnp.eye(c, dtype=a.dtype)
    rhs = jnp.concatenate([v * beta[..., None], k_beta * jnp.exp(g)[..., None]], axis=-1)
    sol = lax.linalg.triangular_solve(t_mat, rhs, left_side=True, lower=True, unit_diagonal=True)
    u_c, w_c = sol[..., :dv], sol[..., dv:]
    qk = jnp.einsum('bhnid,bhnjd->bhnij', q, k) * decay

    def step(state, inp):
        q_i, k_i, u_i, w_i, g_i, qk_i = inp
        v_new = u_i - jnp.einsum('bhcd,bhde->bhce', w_i, state)
        o = (jnp.einsum('bhcd,bhde->bhce', q_i * jnp.exp(g_i)[..., None], state)
             + jnp.einsum('bhcj,bhje->bhce', qk_i, v_new))
        g_last = g_i[..., -1:]
        state = (state * jnp.exp(g_last)[..., None]
                 + jnp.einsum('bhcd,bhce->bhde', k_i * jnp.exp(g_last - g_i)[..., None], v_new))
        return state, o

    xs = tuple(jnp.moveaxis(t, 2, 0) for t in (q, k, u_c, w_c, g, qk))
    state0 = jnp.zeros((b, h, dk, dv), F32)
    _, o = lax.scan(step, state0, xs)
    return jnp.moveaxis(o, 0, 2).reshape(b, h, l, dv)


def gdn_mixer(p, conv_w, a_log, dt_bias, norm_w):
    b, s, _ = p.shape
    qkv = jax.nn.silu(depthwise_conv_centred(p[..., :OFF_Z], conv_w)).astype(F32)
    q, k, v = jnp.split(qkv, 3, axis=-1)
    q = l2_normalize(q.reshape(b, s, GDN_HEADS, GDN_DK)) * (GDN_DK ** -0.5)
    k = l2_normalize(k.reshape(b, s, GDN_HEADS, GDN_DK))
    v = v.reshape(b, s, GDN_HEADS, GDN_DV)
    z = p[..., OFF_Z:OFF_BA].astype(F32).reshape(b, s, GDN_HEADS, GDN_DV)
    ba = p[..., OFF_BA:OFF_HY].astype(F32).reshape(b, s, 2, 2, GDN_HEADS)
    beta = jax.nn.sigmoid(ba[:, :, 0])
    g = -jnp.exp(a_log.astype(F32)) * jax.nn.softplus(ba[:, :, 1] + dt_bias.astype(F32))
    qh, kh, vh = (jnp.transpose(t, (0, 2, 1, 3)) for t in (q, k, v))
    gh = jnp.transpose(g, (0, 2, 3, 1))
    bh = jnp.transpose(beta, (0, 2, 3, 1))
    rev = lambda t: jnp.flip(t, axis=2)
    o_fwd = gated_delta_chunked(qh, kh, vh, gh[:, 0], bh[:, 0])
    o_bwd = rev(gated_delta_chunked(rev(qh), rev(kh), rev(vh), rev(gh[:, 1]), rev(bh[:, 1])))
    o = jnp.transpose(o_fwd + o_bwd, (0, 2, 1, 3))
    o = o * lax.rsqrt(jnp.mean(o * o, axis=-1, keepdims=True) + EPS) * norm_w.astype(F32) * jax.nn.silu(z)
    return o.reshape(b, s, GDN_W).astype(p.dtype)


def hyena_filters(l, w1, b1, w2, b2, w3, b3, w4, b4, freq):
    t = jnp.linspace(0.0, 1.0, l, dtype=F32)[:, None]
    ang = (2.0 * math.pi / l) * jnp.arange(l, dtype=F32)[:, None]
    f = jnp.linspace(1e-4, HY_BANDS - 1, HY_BANDS, dtype=F32)[None, :]
    z = jnp.concatenate([t, jnp.cos(f * ang), -jnp.sin(f * ang)], axis=-1)
    fr = freq.astype(F32)
    hid = jnp.sin(fr * (z @ w1.astype(F32) + b1.astype(F32)))
    hid = jnp.sin(fr * (hid @ w2.astype(F32) + b2.astype(F32)))
    hid = jnp.sin(fr * (hid @ w3.astype(F32) + b3.astype(F32)))
    h = hid @ w4.astype(F32) + b4.astype(F32)
    max_decay = math.log(HY_DECAY_TARGET) / HY_STEEP_DECAY_PCT
    min_decay = math.log(HY_DECAY_TARGET) / HY_SHALLOW_DECAY_PCT
    deltas = jnp.linspace(min_decay, max_decay, HY_W, dtype=F32)
    window = jnp.exp(-t * jnp.abs(deltas)[None, :])
    h_fwd = h[:, :HY_W] * window
    h_bwd = h[:, HY_W:] * window
    buf = jnp.concatenate([h_fwd, jnp.zeros((1, HY_W), F32), h_bwd[:0:-1]], axis=0)
    return buf / jnp.sum(jnp.abs(buf), axis=0, keepdims=True)


def hyena_mixer(p, conv_w, conv_b, w1, b1, w2, b2, w3, b3, w4, b4, freq, skip):
    b, l, _ = p.shape
    u = (depthwise_conv_centred(p[..., OFF_HY:OFF_FN], conv_w) + conv_b.astype(p.dtype)).astype(F32)
    x1, x2, v = jnp.split(u, 3, axis=-1)
    filt = hyena_filters(l, w1, b1, w2, b2, w3, b3, w4, b4, freq)
    n = 2 * l
    vg = v * x2
    y = jnp.fft.irfft(jnp.fft.rfft(vg, n=n, axis=1) * jnp.fft.rfft(filt, n=n, axis=0)[None], n=n, axis=1)[:, :l]
    y = (y + vg * skip.astype(F32)) * x1
    return y.astype(p.dtype)


def fnet_mixer(p, w):
    b, s, _ = p.shape
    a = p[..., OFF_FN:].astype(F32).reshape(b, s, FN_GROUPS, FN_GC)
    mixed = jnp.real(jnp.fft.fft2(a, axes=(1, 3), norm='ortho'))
    y = jnp.einsum('bsgc,gce->bsge', mixed, w.astype(F32))
    return y.reshape(b, s, FN_W).astype(p.dtype)


def peer_ffn(h, wq, k1, k2, u_tab, v_tab):
    b, s, d = h.shape
    q = jnp.einsum('bsd,de->bse', h, wq).reshape(b, s, PEER_HEADS, 2, PEER_HALF)
    s1 = jnp.einsum('bshd,nd->bshn', q[..., 0, :], k1).astype(F32)
    s2 = jnp.einsum('bshd,nd->bshn', q[..., 1, :], k2).astype(F32)
    v1, i1 = lax.top_k(s1, PEER_TOPK)
    v2, i2 = lax.top_k(s2, PEER_TOPK)
    cand = (v1[..., :, None] + v2[..., None, :]).reshape(b, s, PEER_HEADS, PEER_TOPK * PEER_TOPK)
    sc, ci = lax.top_k(cand, PEER_TOPK)
    e1 = jnp.take_along_axis(i1, ci // PEER_TOPK, axis=-1)
    e2 = jnp.take_along_axis(i2, ci % PEER_TOPK, axis=-1)
    expert = (e1 * PEER_NKEYS + e2).reshape(-1, PEER_BLOCK, PEER_HEADS * PEER_TOPK)
    gate = jax.nn.softmax(sc, axis=-1).astype(h.dtype).reshape(-1, PEER_BLOCK, PEER_HEADS * PEER_TOPK)
    tokens = h.reshape(-1, PEER_BLOCK, d)

    def block(args):
        hb, eb, gb = args
        act = jax.nn.gelu(jnp.einsum('td,ted->te', hb, u_tab[eb]), approximate=False)
        return jnp.einsum('te,ted->td', act * gb, v_tab[eb])

    out = lax.map(block, (tokens, expert, gate))
    return out.reshape(b, s, d)


def setup_inputs(seed: int = 0) -> dict:
    key = jax.random.key(seed)
    ks = iter(jax.random.split(key, 40))
    nrm = lambda shape, scale: jax.random.normal(next(ks), shape, F32) * scale
    gain = lambda shape: 1.0 + nrm(shape, 0.01)
    L = DEPTH
    a_log = jnp.log(jax.random.uniform(next(ks), (L, 2, GDN_HEADS), F32, 1.0, 16.0))
    dt = jnp.exp(jax.random.uniform(next(ks), (L, 2, GDN_HEADS), F32, math.log(1e-3), math.log(0.1)))
    dt_bias = dt + jnp.log(-jnp.expm1(-dt))
    return {
        'x': nrm((BATCH, SEQ, D_MODEL), 1.0),
        'norm1_w': gain((L, D_MODEL)),
        'w_in': nrm((L, D_MODEL, IN_COLS), D_MODEL ** -0.5),
        'gdn_conv_w': nrm((L, GDN_CONV, 3 * GDN_W), GDN_CONV ** -0.5),
        'gdn_a_log': a_log,
        'gdn_dt_bias': dt_bias,
        'gdn_norm_w': gain((L, GDN_DV)),
        'hy_conv_w': nrm((L, HY_CONV, 3 * HY_W), HY_CONV ** -0.5),
        'hy_conv_b': nrm((L, 3 * HY_W), 0.02),
        'hy_w1': nrm((L, HY_EMB, HY_FFN), HY_EMB ** -0.5),
        'hy_b1': nrm((L, HY_FFN), 0.02),
        'hy_w2': nrm((L, HY_FFN, HY_FFN), HY_FFN ** -0.5),
        'hy_b2': nrm((L, HY_FFN), 0.02),
        'hy_w3': nrm((L, HY_FFN, HY_FFN), HY_FFN ** -0.5),
        'hy_b3': nrm((L, HY_FFN), 0.02),
        'hy_w4': nrm((L, HY_FFN, 2 * HY_W), HY_FFN ** -0.5),
        'hy_b4': nrm((L, 2 * HY_W), 0.02),
        'hy_freq': gain((L, HY_FFN)),
        'hy_skip': nrm((L, HY_W), 0.1),
        'fnet_w': nrm((L, FN_GROUPS, FN_GC, FN_GC), FN_GC ** -0.5),
        'w_out': nrm((L, D_MIX, D_MODEL), D_MIX ** -0.5),
        'norm2_w': gain((L, D_MODEL)),
        'peer_wq': nrm((L, D_MODEL, PEER_HEADS * PEER_QDIM), D_MODEL ** -0.5),
        'peer_k1': nrm((L, PEER_NKEYS, PEER_HALF), PEER_HALF ** -0.5),
        'peer_k2': nrm((L, PEER_NKEYS, PEER_HALF), PEER_HALF ** -0.5),
        'peer_u': nrm((L, PEER_N, D_MODEL), D_MODEL ** -0.5),
        'peer_v': nrm((L, PEER_N, D_MODEL), PEER_TOPK ** -0.5),
        'final_norm_w': gain((D_MODEL,)),
    }


def reference(x, norm1_w, w_in, gdn_conv_w, gdn_a_log, gdn_dt_bias, gdn_norm_w, hy_conv_w, hy_conv_b,
              hy_w1, hy_b1, hy_w2, hy_b2, hy_w3, hy_b3, hy_w4, hy_b4, hy_freq, hy_skip, fnet_w, w_out,
              norm2_w, peer_wq, peer_k1, peer_k2, peer_u, peer_v, final_norm_w):
    for l in range(DEPTH):
        h = rms_norm(x, norm1_w[l])
        p = jnp.einsum('bsd,dc->bsc', h, w_in[l])
        y_a = gdn_mixer(p, gdn_conv_w[l], gdn_a_log[l], gdn_dt_bias[l], gdn_norm_w[l])
        y_b = hyena_mixer(p, hy_conv_w[l], hy_conv_b[l], hy_w1[l], hy_b1[l], hy_w2[l], hy_b2[l],
                          hy_w3[l], hy_b3[l], hy_w4[l], hy_b4[l], hy_freq[l], hy_skip[l])
        y_c = fnet_mixer(p, fnet_w[l])
        mix = jnp.concatenate([y_a, y_b, y_c], axis=-1)
        x = x + jnp.einsum('bsc,cd->bsd', mix, w_out[l]).astype(x.dtype)
        h = rms_norm(x, norm2_w[l])
        x = x + peer_ffn(h, peer_wq[l], peer_k1[l], peer_k2[l], peer_u[l], peer_v[l]).astype(x.dtype)
    return rms_norm(x, final_norm_w)
```

```python
import functools
import math

import jax
import jax.numpy as jnp
from jax import lax
from jax.experimental import pallas as pl
from jax.experimental.pallas import tpu as pltpu

F32 = jnp.float32
BF16 = jnp.bfloat16
EPS = 1e-6
LANES = 128
VMEM_LIMIT_BYTES = 56 * 1024 * 1024
HIGHEST = lax.Precision.HIGHEST

GDN_HEADS = 8
GDN_D = 128
GDN_W = GDN_HEADS * GDN_D
GDN_CHUNK = 64
HY_W = 512
HY_EMB = 33
HY_BANDS = 16
HY_FFN = 64
FN_W = 512
FN_GROUPS = 4
FN_GC = 128
PEER_HEADS = 8
PEER_NKEYS = 128
PEER_HALF = 128
PEER_TOPK = 16
P_QKV = 0
P_Z = 3 * GDN_W
P_HY = P_Z + GDN_W
P_FN = P_HY + 3 * HY_W
P_BA = P_FN + FN_W
P_COLS = P_BA + LANES


def _cparams(*sem):
    return pltpu.CompilerParams(dimension_semantics=sem, vmem_limit_bytes=VMEM_LIMIT_BYTES)


def _dot(a, b, precision=None):
    return jnp.dot(a, b, preferred_element_type=F32, precision=precision)


def _dot_nt(a, b, precision=None):
    return lax.dot_general(a, b, (((1,), (1,)), ((), ())), preferred_element_type=F32, precision=precision)


def _dot_tn(a, b, precision=None):
    return lax.dot_general(a, b, (((0,), (0,)), ((), ())), preferred_element_type=F32, precision=precision)


def _rms_body(x_ref, w_ref, o_ref):
    x = x_ref[...]
    ms = jnp.mean(x * x, axis=-1, keepdims=True)
    o_ref[...] = (x * lax.rsqrt(ms + EPS) * w_ref[...]).astype(o_ref.dtype)


def rmsnorm(x2d, w, out_dtype):
    n, d = x2d.shape
    tm = min(256, n)
    return pl.pallas_call(
        _rms_body,
        grid=(n // tm,),
        in_specs=[pl.BlockSpec((tm, d), lambda i: (i, 0)), pl.BlockSpec((1, d), lambda i: (0, 0))],
        out_specs=pl.BlockSpec((tm, d), lambda i: (i, 0)),
        out_shape=jax.ShapeDtypeStruct((n, d), out_dtype),
        compiler_params=_cparams("parallel"),
        name="rmsnorm",
    )(x2d, w.reshape(1, d).astype(F32))


def _add_rms_body(x_ref, y_ref, w_ref, *o_refs):
    x = x_ref[...] + y_ref[...]
    ms = jnp.mean(x * x, axis=-1, keepdims=True)
    o_refs[-1][...] = (x * lax.rsqrt(ms + EPS) * w_ref[...]).astype(o_refs[-1].dtype)
    if len(o_refs) == 2:
        o_refs[0][...] = x


def add_rmsnorm(x2d, y2d, w, out_dtype, emit_sum):
    n, d = x2d.shape
    tm = min(256, n)
    row = pl.BlockSpec((tm, d), lambda i: (i, 0))
    normed = jax.ShapeDtypeStruct((n, d), out_dtype)
    return pl.pallas_call(
        _add_rms_body,
        grid=(n // tm,),
        in_specs=[row, row, pl.BlockSpec((1, d), lambda i: (0, 0))],
        out_specs=[row, row] if emit_sum else row,
        out_shape=[jax.ShapeDtypeStruct((n, d), F32), normed] if emit_sum else normed,
        compiler_params=_cparams("parallel"),
        name="add_rmsnorm",
    )(x2d, y2d, w.reshape(1, d).astype(F32))


def _linear_body(*refs, npairs, has_res):
    o_ref = refs[-1]
    acc = _dot(refs[0][...], refs[1][...])
    for p in range(1, npairs):
        acc = acc + _dot(refs[2 * p][...], refs[2 * p + 1][...])
    if has_res:
        acc = acc + refs[2 * npairs][...]
    o_ref[...] = acc.astype(o_ref.dtype)


def linear(pairs, res, out_dtype, tm=512, tn=512):
    m = pairs[0][0].shape[0]
    n = pairs[0][1].shape[1]
    tm = min(tm, m)
    tn = min(tn, n)
    assert m % tm == 0 and n % tn == 0
    in_specs, args = [], []
    for a, b in pairs:
        k = a.shape[1]
        in_specs += [pl.BlockSpec((tm, k), lambda j, i: (i, 0)), pl.BlockSpec((k, tn), lambda j, i: (0, j))]
        args += [a, b]
    if res is not None:
        in_specs.append(pl.BlockSpec((tm, tn), lambda j, i: (i, j)))
        args.append(res)
    return pl.pallas_call(
        functools.partial(_linear_body, npairs=len(pairs), has_res=res is not None),
        grid=(n // tn, m // tm),
        in_specs=in_specs,
        out_specs=pl.BlockSpec((tm, tn), lambda j, i: (i, j)),
        out_shape=jax.ShapeDtypeStruct((m, n), out_dtype),
        compiler_params=_cparams("parallel", "parallel"),
        name="linear",
    )(*args)


def _conv3(x, prev_blk, next_blk, w, i, ns):
    ts = x.shape[0]
    prev_row = jnp.where(i > 0, prev_blk[7:8, :], 0.0)
    next_row = jnp.where(i < ns - 1, next_blk[0:1, :], 0.0)
    rows = lax.broadcasted_iota(jnp.int32, x.shape, 0)
    xm = jnp.where(rows == 0, prev_row, pltpu.roll(x, 1, 0))
    xp = jnp.where(rows == ts - 1, next_row, pltpu.roll(x, ts - 1, 0))
    return xm * w[0:1, :] + x * w[1:2, :] + xp * w[2:3, :]


def _halo_specs(ts, s, col_fn):
    nb8 = s // 8
    r = ts // 8
    cur = pl.BlockSpec((1, ts, LANES), lambda b, i, j: (b, i, col_fn(j)))
    prev = pl.BlockSpec((1, 8, LANES), lambda b, i, j: (b, jnp.maximum(i * r - 1, 0), col_fn(j)))
    nxt = pl.BlockSpec((1, 8, LANES), lambda b, i, j: (b, jnp.minimum((i + 1) * r, nb8 - 1), col_fn(j)))
    return [cur, prev, nxt]


def _gdn_prep_body(x_ref, xp_ref, xn_ref, w_ref, o_ref, *, ns):
    i = pl.program_id(1)
    j = pl.program_id(2)
    y = _conv3(x_ref[0], xp_ref[0], xn_ref[0], w_ref[...], i, ns)
    y = y * jax.nn.sigmoid(y)
    rs = lax.rsqrt(jnp.sum(y * y, axis=-1, keepdims=True) + EPS)
    fac = jnp.where(j < GDN_HEADS, rs * (GDN_D ** -0.5), jnp.where(j < 2 * GDN_HEADS, rs, 1.0))
    o_ref[0] = y * fac


def gdn_prep(p3, conv_w):
    b, s, _ = p3.shape
    ts = min(512, s)
    ns = s // ts
    nblk = 3 * GDN_HEADS
    return pl.pallas_call(
        functools.partial(_gdn_prep_body, ns=ns),
        grid=(b, ns, nblk),
        in_specs=_halo_specs(ts, s, lambda j: j) + [pl.BlockSpec((3, LANES), lambda b_, i, j: (0, j))],
        out_specs=pl.BlockSpec((1, ts, LANES), lambda b_, i, j: (b_, i, j)),
        out_shape=jax.ShapeDtypeStruct((b, s, 3 * GDN_W), F32),
        compiler_params=_cparams("parallel", "parallel", "parallel"),
        name="gdn_prep",
    )(p3, p3, p3, conv_w.astype(F32))


def _gdn_wy_body(alog_ref, dtb_ref, q_ref, k_ref, v_ref, bac_ref, bar_ref,
                 u_ref, w_ref, qk_ref, qg_ref, kg_ref, eg_ref, *, nchunk):
    c_ = GDN_CHUNK
    h = pl.program_id(1)
    ii = lax.broadcasted_iota(jnp.int32, (c_, c_), 0)
    jj = lax.broadcasted_iota(jnp.int32, (c_, c_), 1)
    eye = (ii == jj).astype(F32)
    for d in range(2):
        neg_a = -jnp.exp(jnp.full((1, 1), alog_ref[d, h], F32))
        dtb = dtb_ref[d, h]
        incl = (ii >= jj) if d == 0 else (ii <= jj)
        incl_t = (ii <= jj) if d == 0 else (ii >= jj)
        strict = (ii > jj) if d == 0 else (ii < jj)
        for c in range(nchunk):
            sl = slice(c * c_, (c + 1) * c_)
            q = q_ref[0, sl, :]
            k = k_ref[0, sl, :]
            v = v_ref[0, sl, :]
            beta = jax.nn.sigmoid(bac_ref[0, 0, sl, d:d + 1])
            lg_c = neg_a * jax.nn.softplus(bac_ref[0, 0, sl, 2 + d:3 + d] + dtb)
            lg_r = neg_a * jax.nn.softplus(bar_ref[0, 0, 2 + d:3 + d, sl] + dtb)
            g_c = jnp.sum(jnp.where(incl, lg_r, 0.0), axis=1, keepdims=True)
            g_r = jnp.sum(jnp.where(incl_t, lg_c, 0.0), axis=0, keepdims=True)
            g_tot = jnp.sum(lg_r, axis=1, keepdims=True)
            decay = jnp.where(incl, jnp.exp(jnp.where(incl, g_c - g_r, 0.0)), 0.0)
            kb = k * beta
            a = jnp.where(strict, _dot_nt(kb, k, HIGHEST) * decay, 0.0)
            p = -a
            x = eye + p
            for _ in range(int(math.log2(c_)) - 1):
                p = _dot(p, p, HIGHEST)
                x = x + _dot(x, p, HIGHEST)
            rhs = jnp.concatenate([v * beta, kb * jnp.exp(g_c)], axis=1)
            sol = _dot(x, rhs, HIGHEST)
            u_ref[d, 0, sl, :] = sol[:, :GDN_D]
            w_ref[d, 0, sl, :] = sol[:, GDN_D:].astype(w_ref.dtype)
            qk_ref[d, 0, 0, sl, :] = jnp.where(incl, _dot_nt(q, k, HIGHEST) * decay, 0.0).astype(qk_ref.dtype)
            qg_ref[d, 0, sl, :] = (q * jnp.exp(g_c)).astype(qg_ref.dtype)
            kg_ref[d, 0, sl, :] = (k * jnp.exp(g_tot - g_c)).astype(kg_ref.dtype)
            eg_ref[d, 0, 0, c] = jnp.broadcast_to(jnp.exp(g_tot), (1, LANES))


def gdn_wy(qkv, ba_col, ba_row, a_log, dt_bias):
    b, s, _ = qkv.shape
    hh = GDN_HEADS
    ts = min(256, s)
    nchunk = ts // GDN_CHUNK
    n = s // GDN_CHUNK
    smem = pl.BlockSpec(memory_space=pltpu.SMEM)
    col = lambda off: pl.BlockSpec((1, ts, LANES), lambda b_, h, i: (b_, i, off + h))
    row_out = pl.BlockSpec((2, 1, ts, LANES), lambda b_, h, i: (0, b_, i, h))
    big = lambda dt: jax.ShapeDtypeStruct((2, b, s, GDN_W), dt)
    return pl.pallas_call(
        functools.partial(_gdn_wy_body, nchunk=nchunk),
        grid=(b, hh, s // ts),
        in_specs=[smem, smem, col(0), col(hh), col(2 * hh),
                  pl.BlockSpec((1, 1, ts, 4), lambda b_, h, i: (b_, h, i, 0)),
                  pl.BlockSpec((1, 1, 4, ts), lambda b_, h, i: (b_, h, 0, i))],
        out_specs=[row_out, row_out,
                   pl.BlockSpec((2, 1, 1, ts, GDN_CHUNK), lambda b_, h, i: (0, b_, h, i, 0)),
                   row_out, row_out,
                   pl.BlockSpec((2, 1, 1, nchunk, 1, LANES), lambda b_, h, i: (0, b_, h, i, 0, 0))],
        out_shape=[big(F32), big(BF16), jax.ShapeDtypeStruct((2, b, hh, s, GDN_CHUNK), BF16),
                   big(BF16), big(BF16), jax.ShapeDtypeStruct((2, b, hh, n, 1, LANES), F32)],
        compiler_params=_cparams("parallel", "parallel", "parallel"),
        name="gdn_wy",
    )(a_log.astype(F32), dt_bias.astype(F32), qkv, qkv, qkv, ba_col, ba_row)


def _gdn_scan_body(*refs, hb):
    ins = refs[:12]
    of_ref, ob_ref, s_ref = refs[12], refs[13], refs[14]
    o_refs = (of_ref, ob_ref)

    @pl.when(pl.program_id(2) == 0)
    def _():
        s_ref[...] = jnp.zeros_like(s_ref)

    for d in range(2):
        u_ref, w_ref, qk_ref, qg_ref, kg_ref, eg_ref = ins[6 * d:6 * d + 6]
        for hh in range(hb):
            cs = slice(hh * GDN_D, (hh + 1) * GDN_D)
            st = s_ref[d, hh]
            stb = st.astype(BF16)
            v_new = u_ref[0, 0, :, cs] - _dot(w_ref[0, 0, :, cs], stb)
            v_new_b = v_new.astype(BF16)
            o_refs[d][0, :, cs] = _dot(qg_ref[0, 0, :, cs], stb) + _dot(qk_ref[0, 0, hh], v_new_b)
            s_ref[d, hh] = st * eg_ref[0, 0, hh, 0] + _dot_tn(kg_ref[0, 0, :, cs], v_new_b)


def gdn_scan(u, w, qk, qg, kg, eg):
    _, b, s, _ = u.shape
    c_ = GDN_CHUNK
    n = s // c_
    hb = GDN_HEADS
    def dir_specs(d):
        ci = (lambda i: i) if d == 0 else (lambda i: n - 1 - i)
        row = pl.BlockSpec((1, 1, c_, hb * GDN_D), lambda b_, g, i: (d, b_, ci(i), g))
        return [row, row,
                pl.BlockSpec((1, 1, hb, c_, c_), lambda b_, g, i: (d, b_, g, ci(i), 0)),
                row, row,
                pl.BlockSpec((1, 1, hb, 1, 1, LANES), lambda b_, g, i: (d, b_, g, ci(i), 0, 0))]

    in_specs = dir_specs(0) + dir_specs(1)
    args = [u, w, qk, qg, kg, eg] * 2
    out_f = pl.BlockSpec((1, c_, hb * GDN_D), lambda b_, g, i: (b_, i, g))
    out_b = pl.BlockSpec((1, c_, hb * GDN_D), lambda b_, g, i: (b_, n - 1 - i, g))
    return pl.pallas_call(
        functools.partial(_gdn_scan_body, hb=hb),
        grid=(b, GDN_HEADS // hb, n),
        in_specs=in_specs,
        out_specs=[out_f, out_b],
        out_shape=[jax.ShapeDtypeStruct((b, s, GDN_W), F32)] * 2,
        scratch_shapes=[pltpu.VMEM((2, hb, GDN_D, GDN_D), F32)],
        compiler_params=_cparams("parallel", "parallel", "arbitrary"),
        name="gdn_scan",
    )(*args)


def _gdn_out_body(of_ref, ob_ref, z_ref, nw_ref, o_ref):
    o = of_ref[...] + ob_ref[...]
    z = z_ref[...]
    o = o * lax.rsqrt(jnp.mean(o * o, axis=-1, keepdims=True) + EPS) * nw_ref[...]
    o_ref[...] = (o * (z * jax.nn.sigmoid(z))).astype(o_ref.dtype)


def gdn_out(o_f, o_b, p2, norm_w):
    n, _ = o_f.shape
    ts = min(1024, n)
    blk = lambda off: pl.BlockSpec((ts, LANES), lambda i, h: (i, off + h))
    return pl.pallas_call(
        _gdn_out_body,
        grid=(n // ts, GDN_HEADS),
        in_specs=[blk(0), blk(0), blk(P_Z // LANES), pl.BlockSpec((1, LANES), lambda i, h: (0, 0))],
        out_specs=blk(0),
        out_shape=jax.ShapeDtypeStruct((n, GDN_W), BF16),
        compiler_params=_cparams("parallel", "parallel"),
        name="gdn_out",
    )(o_f, o_b, p2, norm_w.reshape(1, LANES).astype(F32))


def gdn_mixer(p3, conv_w, a_log, dt_bias, norm_w):
    b, s, _ = p3.shape
    qkv = gdn_prep(p3, conv_w)
    ba = p3[..., P_BA:P_BA + 4 * GDN_HEADS].reshape(b, s, 4, GDN_HEADS)
    ba_col = jnp.transpose(ba, (0, 3, 1, 2))
    ba_row = jnp.transpose(ba, (0, 3, 2, 1))
    u, w, qk, qg, kg, eg = gdn_wy(qkv, ba_col, ba_row, a_log, dt_bias)
    o_f, o_b = gdn_scan(u, w, qk, qg, kg, eg)
    return gdn_out(o_f.reshape(b * s, GDN_W), o_b.reshape(b * s, GDN_W), p3.reshape(b * s, P_COLS), norm_w)


def _angle_tables(rows, cols, period):
    k = lax.broadcasted_iota(jnp.int32, (rows, cols), 0)
    s = lax.broadcasted_iota(jnp.int32, (rows, cols), 1)
    ang = ((k * s) % period).astype(F32) * (2.0 * math.pi / period)
    return k, s, jnp.cos(ang), jnp.sin(ang)


def hyena_tables(s):
    k, t, c, sn = _angle_tables(s, s, 2 * s)
    alt = lambda idx: (1 - 2 * (idx % 2)).astype(F32)
    fwd = jnp.stack([c, jnp.where(k == 0, alt(t), sn)]).astype(BF16)
    inv = jnp.stack([c, jnp.where(t == 0, alt(k), sn)]).astype(BF16)
    return fwd, inv


def fnet_tables(s):
    _, _, c, sn = _angle_tables(s, s, s)
    return jnp.stack([c, sn]).astype(BF16)


def _seq_dft_body(*refs, nk, epilogue):
    t_ref, x_ref = refs[0], refs[1]
    acc_ref = refs[-1]
    o_ref = refs[-2]
    extra = refs[2:-2]
    kk = pl.program_id(2)

    @pl.when(kk == 0)
    def _():
        acc_ref[...] = jnp.zeros_like(acc_ref)

    xb = x_ref[0].astype(BF16)
    acc_ref[0] += _dot(t_ref[0], xb)
    acc_ref[1] += _dot(t_ref[1], xb)

    @pl.when(kk == nk - 1)
    def _():
        epilogue(acc_ref, extra, o_ref)


def _seq_dft(table, x, wd, col_blk, extra_args, extra_specs, epilogue, out_shape, out_spec, tf, tk, name):
    b, s, _ = x.shape
    tf = min(tf, s)
    tk = min(tk, s)
    nk = s // tk
    return pl.pallas_call(
        functools.partial(_seq_dft_body, nk=nk, epilogue=epilogue),
        grid=(b, s // tf, nk),
        in_specs=[pl.BlockSpec((2, tf, tk), lambda b_, f, kk: (0, f, kk)),
                  pl.BlockSpec((1, tk, wd), lambda b_, f, kk: (b_, kk, col_blk))] + extra_specs,
        out_specs=out_spec,
        out_shape=out_shape,
        scratch_shapes=[pltpu.VMEM((2, tf, wd), F32)],
        compiler_params=_cparams("parallel", "parallel", "arbitrary"),
        name=name,
    )(table, x, *extra_args)


def _hy_prep_body(*refs, ns):
    i = pl.program_id(1)
    w_ref, b_ref = refs[9], refs[10]
    vg_ref, x1_ref = refs[11], refs[12]
    outs = []
    for part in range(3):
        x_ref, xp_ref, xn_ref = refs[3 * part:3 * part + 3]
        outs.append(_conv3(x_ref[0], xp_ref[0], xn_ref[0], w_ref[part], i, ns) + b_ref[part])
    x1, x2, v = outs
    vg_ref[0] = v * x2
    x1_ref[0] = x1


def hy_prep(p3, conv_w, conv_b):
    b, s, _ = p3.shape
    ts = min(512, s)
    ns = s // ts
    ng = HY_W // LANES
    base = P_HY // LANES
    in_specs = []
    for part in range(3):
        in_specs += _halo_specs(ts, s, functools.partial(lambda part_, j: base + part_ * ng + j, part))
    in_specs += [pl.BlockSpec((3, 3, LANES), lambda b_, i, j: (0, 0, j)),
                 pl.BlockSpec((3, 1, LANES), lambda b_, i, j: (0, 0, j))]
    out = pl.BlockSpec((1, ts, LANES), lambda b_, i, j: (b_, i, j))
    cw = jnp.transpose(conv_w.astype(F32).reshape(3, 3, HY_W), (1, 0, 2))
    cb = conv_b.astype(F32).reshape(3, 1, HY_W)
    return pl.pallas_call(
        functools.partial(_hy_prep_body, ns=ns),
        grid=(b, ns, ng),
        in_specs=in_specs,
        out_specs=[out, out],
        out_shape=[jax.ShapeDtypeStruct((b, s, HY_W), F32)] * 2,
        compiler_params=_cparams("parallel", "parallel", "parallel"),
        name="hy_prep",
    )(*([p3] * 9), cw, cb)


def _hy_filter_body(z_ref, w1_ref, b1_ref, w2_ref, b2_ref, w3_ref, b3_ref, w4_ref, b4_ref, fr_ref, dl_ref,
                    buf_ref, sum_ref, *, s, tr):
    i = pl.program_id(0)
    z = z_ref[...]
    fr = fr_ref[...]
    hid = jnp.sin(fr * (_dot(z, w1_ref[...], HIGHEST) + b1_ref[...]))
    hid = jnp.sin(fr * (_dot(hid, w2_ref[...], HIGHEST) + b2_ref[...]))
    hid = jnp.sin(fr * (_dot(hid, w3_ref[...], HIGHEST) + b3_ref[...]))
    hh = _dot(hid, w4_ref[...], HIGHEST) + b4_ref[...]
    window = jnp.exp(-z[:, 0:1] * dl_ref[...])
    row = i * tr + lax.broadcasted_iota(jnp.int32, (tr, HY_W), 0)
    val = jnp.where(row < s, hh[:, :HY_W], hh[:, HY_W:]) * window
    val = jnp.where(row == s, 0.0, val)
    buf_ref[...] = val

    @pl.when(i == 0)
    def _():
        sum_ref[...] = jnp.zeros_like(sum_ref)

    sum_ref[...] += jnp.sum(jnp.abs(val), axis=0, keepdims=True)


def hy_filter(s, w1, b1, w2, b2, w3, b3, w4, b4, freq):
    n2 = 2 * s
    slot = jnp.arange(n2)
    pos = jnp.where(slot <= s, jnp.minimum(slot, s - 1), n2 - slot)
    t_all = jnp.linspace(0.0, 1.0, s, dtype=F32)
    ang_all = (2.0 * math.pi / s) * jnp.arange(s, dtype=F32)
    t = t_all[pos][:, None]
    ang = ang_all[pos][:, None]
    f = jnp.linspace(1e-4, HY_BANDS - 1, HY_BANDS, dtype=F32)[None, :]
    z = jnp.concatenate([t, jnp.cos(f * ang), -jnp.sin(f * ang)], axis=-1)
    z = jnp.pad(z, ((0, 0), (0, LANES - HY_EMB)))
    w1p = jnp.pad(w1.astype(F32), ((0, LANES - HY_EMB), (0, 0)))
    max_decay = math.log(1e-2) / 0.3
    min_decay = math.log(1e-2) / 1.5
    deltas = jnp.abs(jnp.linspace(min_decay, max_decay, HY_W, dtype=F32))[None, :]
    tr = min(512, n2)
    full = lambda a: pl.BlockSpec(a.shape, lambda i: (0,) * a.ndim)
    r2 = lambda a: a.astype(F32).reshape(1, -1)
    args = [z, w1p, r2(b1), w2.astype(F32), r2(b2), w3.astype(F32), r2(b3), w4.astype(F32), r2(b4), r2(freq), deltas]
    return pl.pallas_call(
        functools.partial(_hy_filter_body, s=s, tr=tr),
        grid=(n2 // tr,),
        in_specs=[pl.BlockSpec((tr, LANES), lambda i: (i, 0))] + [full(a) for a in args[1:]],
        out_specs=[pl.BlockSpec((tr, HY_W), lambda i: (i, 0)), pl.BlockSpec((1, HY_W), lambda i: (0, 0))],
        out_shape=[jax.ShapeDtypeStruct((n2, HY_W), F32), jax.ShapeDtypeStruct((1, HY_W), F32)],
        compiler_params=_cparams("arbitrary"),
        name="hy_filter",
    )(*args)


def _spec_raw_epilogue(acc_ref, extra, o_ref):
    o_ref[0] = acc_ref[...]


def _hy_spec_combine_body(a_ref, sum_ref, o_ref, *, tf):
    f = pl.program_id(0)
    k = f * tf + lax.broadcasted_iota(jnp.int32, (tf, HY_W), 0)
    sign = (1 - 2 * (k % 2)).astype(F32)
    inv = 1.0 / sum_ref[...]
    first = a_ref[0]
    second = a_ref[1]
    o_ref[0] = (first[0] + sign * second[0]) * inv
    sign1 = jnp.where(k == 0, 1.0, sign)
    o_ref[1] = (first[1] + sign1 * second[1]) * inv


def hy_filter_spectrum(buf, colsum, fwd_table, s):
    halves = buf.reshape(2, s, HY_W)
    tf = min(512, s)
    raw = _seq_dft(fwd_table, halves, HY_W, 0, [], [], _spec_raw_epilogue,
                   jax.ShapeDtypeStruct((2, 2, s, HY_W), F32),
                   pl.BlockSpec((1, 2, tf, HY_W), lambda b_, f, kk: (b_, 0, f, 0)), tf, 512, "hy_filter_dft")
    return pl.pallas_call(
        functools.partial(_hy_spec_combine_body, tf=tf),
        grid=(s // tf,),
        in_specs=[pl.BlockSpec((2, 2, tf, HY_W), lambda f: (0, 0, f, 0)), pl.BlockSpec((1, HY_W), lambda f: (0, 0))],
        out_specs=pl.BlockSpec((2, tf, HY_W), lambda f: (0, f, 0)),
        out_shape=jax.ShapeDtypeStruct((2, s, HY_W), F32),
        compiler_params=_cparams("parallel"),
        name="hy_filter_combine",
    )(raw, colsum)


def _hy_fwd_epilogue(acc_ref, extra, o_ref, *, s, tf):
    h_ref = extra[0]
    f = pl.program_id(1)
    xc, xs = acc_ref[0], acc_ref[1]
    hc, hs = h_ref[0], h_ref[1]
    row0 = (f * tf + lax.broadcasted_iota(jnp.int32, xc.shape, 0)) == 0
    scale = jnp.where(row0, 1.0 / (2 * s), 2.0 / (2 * s))
    yc = jnp.where(row0, xc * hc, xc * hc - xs * hs) * scale
    ys = jnp.where(row0, xs * hs, xc * hs + xs * hc) * scale
    o_ref[0, 0] = yc.astype(o_ref.dtype)
    o_ref[0, 1] = ys.astype(o_ref.dtype)


def _hy_inv_body(t_ref, y_ref, vg_ref, x1_ref, skip_ref, o_ref, acc_ref, *, nk):
    kk = pl.program_id(2)

    @pl.when(kk == 0)
    def _():
        acc_ref[...] = jnp.zeros_like(acc_ref)

    acc_ref[...] += _dot(t_ref[0], y_ref[0, 0]) + _dot(t_ref[1], y_ref[0, 1])

    @pl.when(kk == nk - 1)
    def _():
        o_ref[0] = ((acc_ref[...] + vg_ref[0] * skip_ref[...]) * x1_ref[0]).astype(o_ref.dtype)


def hyena_mixer(p3, conv_w, conv_b, spectrum, skip, fwd_table, inv_table):
    b, s, _ = p3.shape
    vg, x1 = hy_prep(p3, conv_w, conv_b)
    tf = min(512, s)
    tk = min(512, s)
    yp = _seq_dft(fwd_table, vg, HY_W, 0, [spectrum], [pl.BlockSpec((2, tf, HY_W), lambda b_, f, kk: (0, f, 0))],
                  functools.partial(_hy_fwd_epilogue, s=s, tf=tf),
                  jax.ShapeDtypeStruct((b, 2, s, HY_W), BF16),
                  pl.BlockSpec((1, 2, tf, HY_W), lambda b_, f, kk: (b_, 0, f, 0)), tf, tk, "hy_fwd_dft")
    nk = s // tk
    tile = pl.BlockSpec((1, tf, HY_W), lambda b_, f, kk: (b_, f, 0))
    return pl.pallas_call(
        functools.partial(_hy_inv_body, nk=nk),
        grid=(b, s // tf, nk),
        in_specs=[pl.BlockSpec((2, tf, tk), lambda b_, f, kk: (0, f, kk)),
                  pl.BlockSpec((1, 2, tk, HY_W), lambda b_, f, kk: (b_, 0, kk, 0)),
                  tile, tile, pl.BlockSpec((1, HY_W), lambda b_, f, kk: (0, 0))],
        out_specs=tile,
        out_shape=jax.ShapeDtypeStruct((b, s, HY_W), BF16),
        scratch_shapes=[pltpu.VMEM((tf, HY_W), F32)],
        compiler_params=_cparams("parallel", "parallel", "arbitrary"),
        name="hy_inv_dft",
    )(inv_table, yp, vg, x1, skip.astype(F32).reshape(1, HY_W))


def _fnet_epilogue(acc_ref, extra, o_ref):
    cw_ref = extra[0]
    for g in range(FN_GROUPS):
        cs = slice(g * FN_GC, (g + 1) * FN_GC)
        y = _dot(acc_ref[0, :, cs].astype(BF16), cw_ref[0, g]) + _dot(acc_ref[1, :, cs].astype(BF16), cw_ref[1, g])
        o_ref[0, :, cs] = y.astype(o_ref.dtype)


def _fnet_fold_body(t_ref, w_ref, o_ref, *, scale):
    for g in range(FN_GROUPS):
        o_ref[0, g] = (_dot(t_ref[0], w_ref[g], HIGHEST) * scale).astype(o_ref.dtype)
        o_ref[1, g] = (_dot(t_ref[1], w_ref[g], HIGHEST) * (-scale)).astype(o_ref.dtype)


def fnet_mixer(p3, w, table):
    b, s, _ = p3.shape
    _, _, cc, sc = _angle_tables(FN_GC, FN_GC, FN_GC)
    folded = pl.pallas_call(
        functools.partial(_fnet_fold_body, scale=1.0 / math.sqrt(s * FN_GC)),
        out_shape=jax.ShapeDtypeStruct((2, FN_GROUPS, FN_GC, FN_GC), BF16),
        name="fnet_fold",
    )(jnp.stack([cc, sc]), w.astype(F32))
    tf = min(512, s)
    return _seq_dft(table, p3, FN_W, P_FN // FN_W, [folded],
                    [pl.BlockSpec((2, FN_GROUPS, FN_GC, FN_GC), lambda b_, f, kk: (0, 0, 0, 0))],
                    _fnet_epilogue, jax.ShapeDtypeStruct((b, s, FN_W), BF16),
                    pl.BlockSpec((1, tf, FN_W), lambda b_, f, kk: (b_, f, 0)), tf, 512, "fnet_dft")


def _topk_rows(x, k, out_ref):
    rows = lax.broadcasted_iota(jnp.int32, x.shape, 0)
    nrows = x.shape[0]

    def step(r, cur):
        m = jnp.max(cur, axis=0, keepdims=True)
        first = jnp.min(jnp.where(cur == m, rows, nrows), axis=0, keepdims=True)
        out_ref[pl.ds(r, 1), :] = m
        return jnp.where(rows == first, -jnp.inf, cur)

    lax.fori_loop(0, k, step, x)


def _peer_score_body(q_ref, k1_ref, k2_ref, a1_ref, a2_ref, e1_ref, e2_ref, tau_ref, v1_ref, v2_ref, sc_ref):
    kk = PEER_TOPK
    q = q_ref[...]
    q1 = q[:, :PEER_HALF].astype(BF16)
    q2 = q[:, PEER_HALF:].astype(BF16)
    s1 = _dot_nt(k1_ref[...], q1)
    s2 = _dot_nt(k2_ref[...], q2)
    _topk_rows(s1, kk, v1_ref)
    _topk_rows(s2, kk, v2_ref)
    v2 = v2_ref[...]
    cand = jnp.concatenate([v1_ref[i:i + 1, :] + v2 for i in range(kk)], axis=0)
    _topk_rows(cand, kk, sc_ref)
    sc = sc_ref[...]
    z = jnp.sum(jnp.exp(sc - sc[0:1, :]), axis=0, keepdims=True)
    a1_ref[0] = s1
    a2_ref[0] = s2
    e1_ref[0] = jnp.exp(s1 - v1_ref[0:1, :]) / z
    e2_ref[0] = jnp.exp(s2 - v2[0:1, :])
    tau_ref[0] = sc[kk - 1:kk, :]


def peer_scores(q, k1, k2):
    n, _ = q.shape
    tt = min(512, n)
    kk = PEER_TOPK
    big = jax.ShapeDtypeStruct((PEER_HEADS, PEER_NKEYS, n), F32)
    blk = pl.BlockSpec((1, PEER_NKEYS, tt), lambda i, h: (h, 0, i))
    keys = pl.BlockSpec((PEER_NKEYS, PEER_HALF), lambda i, h: (0, 0))
    return pl.pallas_call(
        _peer_score_body,
        grid=(n // tt, PEER_HEADS),
        in_specs=[pl.BlockSpec((tt, 2 * PEER_HALF), lambda i, h: (i, h)), keys, keys],
        out_specs=[blk, blk, blk, blk, pl.BlockSpec((1, 1, tt), lambda i, h: (h, 0, i))],
        out_shape=[big, big, big, big, jax.ShapeDtypeStruct((PEER_HEADS, 1, n), F32)],
        scratch_shapes=[pltpu.VMEM((kk, tt), F32), pltpu.VMEM((kk, tt), F32), pltpu.VMEM((kk, tt), F32)],
        compiler_params=_cparams("parallel", "parallel"),
        name="peer_scores",
    )(q, k1.astype(BF16), k2.astype(BF16))


def _peer_dense_body(h_ref, u_ref, vt_ref, a1_ref, a2_ref, e1_ref, e2_ref, tau_ref, o_ref, acc_ref, *, ne, r):
    e = pl.program_id(1)

    @pl.when(e == 0)
    def _():
        acc_ref[...] = jnp.zeros_like(acc_ref)

    act = _dot_nt(u_ref[...], h_ref[...])
    act = 0.5 * act * (1.0 + lax.erf(act * (1.0 / math.sqrt(2.0))))
    gates = []
    for rr in range(r):
        e1 = e * r + rr
        g = None
        for hd in range(PEER_HEADS):
            s1row = a1_ref[hd, pl.ds(e1, 1), :]
            e1row = e1_ref[hd, pl.ds(e1, 1), :]
            sel = (s1row + a2_ref[hd]) >= tau_ref[hd]
            term = jnp.where(sel, e2_ref[hd] * e1row, 0.0)
            g = term if g is None else g + term
        gates.append(g)
    gate = jnp.concatenate(gates, axis=0) if r > 1 else gates[0]
    acc_ref[...] += _dot(vt_ref[...], (act * gate).astype(BF16))

    @pl.when(e == ne - 1)
    def _():
        o_ref[...] = acc_ref[...].T


def peer_dense(h, u_bf, vt_bf, a1, a2, e1, e2, tau):
    n, d = h.shape
    ne_total = u_bf.shape[0]
    tt = min(512, n)
    r = 4
    te = r * PEER_NKEYS
    ne = ne_total // te
    tok = pl.BlockSpec((PEER_HEADS, PEER_NKEYS, tt), lambda i, e: (0, 0, i))
    return pl.pallas_call(
        functools.partial(_peer_dense_body, ne=ne, r=r),
        grid=(n // tt, ne),
        in_specs=[pl.BlockSpec((tt, d), lambda i, e: (i, 0)),
                  pl.BlockSpec((te, d), lambda i, e: (e, 0)),
                  pl.BlockSpec((d, te), lambda i, e: (0, e)),
                  tok, tok, tok, tok,
                  pl.BlockSpec((PEER_HEADS, 1, tt), lambda i, e: (0, 0, i))],
        out_specs=pl.BlockSpec((tt, d), lambda i, e: (i, 0)),
        out_shape=jax.ShapeDtypeStruct((n, d), F32),
        scratch_shapes=[pltpu.VMEM((d, tt), F32)],
        compiler_params=_cparams("parallel", "arbitrary"),
        name="peer_dense",
    )(h, u_bf, vt_bf, a1, a2, e1, e2, tau)


def peer_ffn(x2d, norm_w, wq, k1, k2, u_tab, v_tab):
    h = rmsnorm(x2d, norm_w, BF16)
    q = linear([(h, wq.astype(BF16))], None, F32)
    a1, a2, e1, e2, tau = peer_scores(q, k1, k2)
    return peer_dense(h, u_tab.astype(BF16), v_tab.astype(BF16).T, a1, a2, e1, e2, tau)


def _reorder_w_in(w):
    off_z, off_ba = 3 * GDN_W, 4 * GDN_W
    off_hy = off_ba + 4 * GDN_HEADS
    off_fn = off_hy + 3 * HY_W
    pad = jnp.zeros((w.shape[0], LANES - 4 * GDN_HEADS), w.dtype)
    return jnp.concatenate([w[:, :off_z], w[:, off_z:off_ba], w[:, off_hy:off_fn], w[:, off_fn:],
                            w[:, off_ba:off_hy], pad], axis=1)


def kernel(x, norm1_w, w_in, gdn_conv_w, gdn_a_log, gdn_dt_bias, gdn_norm_w, hy_conv_w, hy_conv_b, hy_w1, hy_b1, hy_w2, hy_b2, hy_w3, hy_b3, hy_w4, hy_b4, hy_freq, hy_skip, fnet_w, w_out, norm2_w, peer_wq, peer_k1, peer_k2, peer_u, peer_v, final_norm_w):
    b, s, d = x.shape
    n = b * s
    depth = w_in.shape[0]
    hy_fwd, hy_inv = hyena_tables(s)
    fn_tab = fnet_tables(s)
    x2d = x.reshape(n, d)
    delta = None
    for l in range(depth):
        if delta is None:
            h = rmsnorm(x2d, norm1_w[l], BF16)
        else:
            x2d, h = add_rmsnorm(x2d, delta, norm1_w[l], BF16, True)
        p = linear([(h, _reorder_w_in(w_in[l]).astype(BF16))], None, F32, tm=512, tn=896)
        p3 = p.reshape(b, s, P_COLS)
        y_a = gdn_mixer(p3, gdn_conv_w[l], gdn_a_log[l], gdn_dt_bias[l], gdn_norm_w[l])
        buf, colsum = hy_filter(s, hy_w1[l], hy_b1[l], hy_w2[l], hy_b2[l], hy_w3[l], hy_b3[l], hy_w4[l], hy_b4[l],
                                hy_freq[l])
        spectrum = hy_filter_spectrum(buf, colsum, hy_fwd, s)
        y_b = hyena_mixer(p3, hy_conv_w[l], hy_conv_b[l], spectrum, hy_skip[l], hy_fwd, hy_inv)
        y_c = fnet_mixer(p3, fnet_w[l], fn_tab)
        wo = w_out[l].astype(BF16)
        x2d = linear([(y_a, wo[:GDN_W]), (y_b.reshape(n, HY_W), wo[GDN_W:GDN_W + HY_W]),
                      (y_c.reshape(n, FN_W), wo[GDN_W + HY_W:])], x2d, F32)
        delta = peer_ffn(x2d, norm2_w[l], peer_wq[l], peer_k1[l], peer_k2[l], peer_u[l], peer_v[l])
    return add_rmsnorm(x2d, delta, final_norm_w, F32, False).reshape(b, s, d)
```

```python
import functools
import math

import jax
import jax.numpy as jnp
from jax import lax
from jax.experimental import pallas as pl
from jax.experimental.pallas import tpu as pltpu

F32 = jnp.float32
BF16 = jnp.bfloat16
EPS = 1e-6
LANES = 128
VMEM_LIMIT_BYTES = 56 * 1024 * 1024
HIGHEST = lax.Precision.HIGHEST

GDN_HEADS = 8
GDN_D = 128
GDN_W = GDN_HEADS * GDN_D
GDN_CHUNK = 64
HY_W = 512
HY_EMB = 33
HY_BANDS = 16
HY_FFN = 64
FN_W = 512
FN_GROUPS = 4
FN_GC = 128
PEER_HEADS = 8
PEER_NKEYS = 128
PEER_HALF = 128
PEER_TOPK = 16
P_QKV = 0
P_Z = 3 * GDN_W
P_HY = P_Z + GDN_W
P_FN = P_HY + 3 * HY_W
P_BA = P_FN + FN_W
P_COLS = P_BA + LANES


def _cparams(*sem):
    return pltpu.CompilerParams(dimension_semantics=sem, vmem_limit_bytes=VMEM_LIMIT_BYTES)


def _dot(a, b, precision=None):
    return jnp.dot(a, b, preferred_element_type=F32, precision=precision)


def _dot_nt(a, b, precision=None):
    return lax.dot_general(a, b, (((1,), (1,)), ((), ())), preferred_element_type=F32, precision=precision)


def _dot_tn(a, b, precision=None):
    return lax.dot_general(a, b, (((0,), (0,)), ((), ())), preferred_element_type=F32, precision=precision)


def _rms_body(x_ref, w_ref, o_ref):
    x = x_ref[...]
    ms = jnp.mean(x * x, axis=-1, keepdims=True)
    o_ref[...] = (x * lax.rsqrt(ms + EPS) * w_ref[...]).astype(o_ref.dtype)


def rmsnorm(x2d, w, out_dtype):
    n, d = x2d.shape
    tm = min(256, n)
    return pl.pallas_call(
        _rms_body,
        grid=(n // tm,),
        in_specs=[pl.BlockSpec((tm, d), lambda i: (i, 0)), pl.BlockSpec((1, d), lambda i: (0, 0))],
        out_specs=pl.BlockSpec((tm, d), lambda i: (i, 0)),
        out_shape=jax.ShapeDtypeStruct((n, d), out_dtype),
        compiler_params=_cparams("parallel"),
        name="rmsnorm",
    )(x2d, w.reshape(1, d).astype(F32))


def _add_rms_body(x_ref, y_ref, w_ref, *o_refs):
    x = x_ref[...] + y_ref[...]
    ms = jnp.mean(x * x, axis=-1, keepdims=True)
    o_refs[-1][...] = (x * lax.rsqrt(ms + EPS) * w_ref[...]).astype(o_refs[-1].dtype)
    if len(o_refs) == 2:
        o_refs[0][...] = x


def add_rmsnorm(x2d, y2d, w, out_dtype, emit_sum):
    n, d = x2d.shape
    tm = min(256, n)
    row = pl.BlockSpec((tm, d), lambda i: (i, 0))
    normed = jax.ShapeDtypeStruct((n, d), out_dtype)
    return pl.pallas_call(
        _add_rms_body,
        grid=(n // tm,),
        in_specs=[row, row, pl.BlockSpec((1, d), lambda i: (0, 0))],
        out_specs=[row, row] if emit_sum else row,
        out_shape=[jax.ShapeDtypeStruct((n, d), F32), normed] if emit_sum else normed,
        compiler_params=_cparams("parallel"),
        name="add_rmsnorm",
    )(x2d, y2d, w.reshape(1, d).astype(F32))


def _linear_body(*refs, npairs, has_res):
    o_ref = refs[-1]
    acc = _dot(refs[0][...], refs[1][...])
    for p in range(1, npairs):
        acc = acc + _dot(refs[2 * p][...], refs[2 * p + 1][...])
    if has_res:
        acc = acc + refs[2 * npairs][...]
    o_ref[...] = acc.astype(o_ref.dtype)


def linear(pairs, res, out_dtype, tm=512, tn=512):
    m = pairs[0][0].shape[0]
    n = pairs[0][1].shape[1]
    tm = min(tm, m)
    tn = min(tn, n)
    assert m % tm == 0 and n % tn == 0
    in_specs, args = [], []
    for a, b in pairs:
        k = a.shape[1]
        in_specs += [pl.BlockSpec((tm, k), lambda j, i: (i, 0)), pl.BlockSpec((k, tn), lambda j, i: (0, j))]
        args += [a, b]
    if res is not None:
        in_specs.append(pl.BlockSpec((tm, tn), lambda j, i: (i, j)))
        args.append(res)
    return pl.pallas_call(
        functools.partial(_linear_body, npairs=len(pairs), has_res=res is not None),
        grid=(n // tn, m // tm),
        in_specs=in_specs,
        out_specs=pl.BlockSpec((tm, tn), lambda j, i: (i, j)),
        out_shape=jax.ShapeDtypeStruct((m, n), out_dtype),
        compiler_params=_cparams("parallel", "parallel"),
        name="linear",
    )(*args)


def _conv3(x, prev_blk, next_blk, w, i, ns):
    ts = x.shape[0]
    prev_row = jnp.where(i > 0, prev_blk[7:8, :], 0.0)
    next_row = jnp.where(i < ns - 1, next_blk[0:1, :], 0.0)
    rows = lax.broadcasted_iota(jnp.int32, x.shape, 0)
    xm = jnp.where(rows == 0, prev_row, pltpu.roll(x, 1, 0))
    xp = jnp.where(rows == ts - 1, next_row, pltpu.roll(x, ts - 1, 0))
    return xm * w[0:1, :] + x * w[1:2, :] + xp * w[2:3, :]


def _halo_specs(ts, s, col_fn):
    nb8 = s // 8
    r = ts // 8
    cur = pl.BlockSpec((1, ts, LANES), lambda b, i, j: (b, i, col_fn(j)))
    prev = pl.BlockSpec((1, 8, LANES), lambda b, i, j: (b, jnp.maximum(i * r - 1, 0), col_fn(j)))
    nxt = pl.BlockSpec((1, 8, LANES), lambda b, i, j: (b, jnp.minimum((i + 1) * r, nb8 - 1), col_fn(j)))
    return [cur, prev, nxt]


def _gdn_prep_body(x_ref, xp_ref, xn_ref, w_ref, o_ref, *, ns):
    i = pl.program_id(1)
    j = pl.program_id(2)
    y = _conv3(x_ref[0], xp_ref[0], xn_ref[0], w_ref[...], i, ns)
    y = y * jax.nn.sigmoid(y)
    rs = lax.rsqrt(jnp.sum(y * y, axis=-1, keepdims=True) + EPS)
    fac = jnp.where(j < GDN_HEADS, rs * (GDN_D ** -0.5), jnp.where(j < 2 * GDN_HEADS, rs, 1.0))
    o_ref[0] = y * fac


def gdn_prep(p3, conv_w):
    b, s, _ = p3.shape
    ts = min(512, s)
    ns = s // ts
    nblk = 3 * GDN_HEADS
    return pl.pallas_call(
        functools.partial(_gdn_prep_body, ns=ns),
        grid=(b, ns, nblk),
        in_specs=_halo_specs(ts, s, lambda j: j) + [pl.BlockSpec((3, LANES), lambda b_, i, j: (0, j))],
        out_specs=pl.BlockSpec((1, ts, LANES), lambda b_, i, j: (b_, i, j)),
        out_shape=jax.ShapeDtypeStruct((b, s, 3 * GDN_W), F32),
        compiler_params=_cparams("parallel", "parallel", "parallel"),
        name="gdn_prep",
    )(p3, p3, p3, conv_w.astype(F32))


def _gdn_wy_body(alog_ref, dtb_ref, q_ref, k_ref, v_ref, bac_ref, bar_ref,
                 u_ref, w_ref, qk_ref, qg_ref, kg_ref, eg_ref, *, nchunk):
    c_ = GDN_CHUNK
    h = pl.program_id(1)
    ii = lax.broadcasted_iota(jnp.int32, (c_, c_), 0)
    jj = lax.broadcasted_iota(jnp.int32, (c_, c_), 1)
    eye = (ii == jj).astype(F32)
    bdot = lambda a, b: _dot(a.astype(BF16), b.astype(BF16))
    gram = []
    for c in range(nchunk):
        sl = slice(c * c_, (c + 1) * c_)
        kb16 = k_ref[0, sl, :].astype(BF16)
        qk16 = jnp.concatenate([q_ref[0, sl, :].astype(BF16), kb16], axis=0)
        gram.append(_dot_nt(qk16, kb16))
    jobs = []
    for d in range(2):
        neg_a = -jnp.exp(jnp.full((1, 1), alog_ref[d, h], F32))
        dtb = dtb_ref[d, h]
        incl = (ii >= jj) if d == 0 else (ii <= jj)
        incl_t = (ii <= jj) if d == 0 else (ii >= jj)
        strict = (ii > jj) if d == 0 else (ii < jj)
        for c in range(nchunk):
            sl = slice(c * c_, (c + 1) * c_)
            beta = jax.nn.sigmoid(bac_ref[0, 0, sl, d:d + 1])
            lg_c = neg_a * jax.nn.softplus(bac_ref[0, 0, sl, 2 + d:3 + d] + dtb)
            lg_r = neg_a * jax.nn.softplus(bar_ref[0, 0, 2 + d:3 + d, sl] + dtb)
            g_c = jnp.sum(jnp.where(incl, lg_r, 0.0), axis=1, keepdims=True)
            g_r = jnp.sum(jnp.where(incl_t, lg_c, 0.0), axis=0, keepdims=True)
            g_tot = jnp.sum(lg_r, axis=1, keepdims=True)
            decay = jnp.where(incl, jnp.exp(jnp.where(incl, g_c - g_r, 0.0)), 0.0)
            qk_ref[d, 0, 0, sl, :] = jnp.where(incl, gram[c][:c_] * decay, 0.0).astype(qk_ref.dtype)
            eg_ref[d, 0, 0, c] = jnp.broadcast_to(jnp.exp(g_tot), (1, LANES))
            p = jnp.where(strict, gram[c][c_:] * decay, 0.0) * (-beta)
            jobs.append(dict(d=d, sl=sl, beta=beta, g_c=g_c, g_tot=g_tot, p=p, x=eye + p))
    for _ in range(int(math.log2(c_)) - 1):
        for jb in jobs:
            jb["p"] = bdot(jb["p"], jb["p"])
        for jb in jobs:
            jb["x"] = jb["x"] + bdot(jb["x"], jb["p"])
    for jb in jobs:
        d, sl, beta, g_c = jb["d"], jb["sl"], jb["beta"], jb["g_c"]
        k = k_ref[0, sl, :]
        kb = k * beta
        rhs = jnp.concatenate([v_ref[0, sl, :] * beta, kb * jnp.exp(g_c)], axis=1)
        sol = bdot(jb["x"], rhs)
        u_ref[d, 0, sl, :] = sol[:, :GDN_D]
        w_ref[d, 0, sl, :] = sol[:, GDN_D:].astype(w_ref.dtype)
        qg_ref[d, 0, sl, :] = (q_ref[0, sl, :] * jnp.exp(g_c)).astype(qg_ref.dtype)
        kg_ref[d, 0, sl, :] = (k * jnp.exp(jb["g_tot"] - g_c)).astype(kg_ref.dtype)


def gdn_wy(qkv, ba_col, ba_row, a_log, dt_bias):
    b, s, _ = qkv.shape
    hh = GDN_HEADS
    ts = min(256, s)
    nchunk = ts // GDN_CHUNK
    n = s // GDN_CHUNK
    smem = pl.BlockSpec(memory_space=pltpu.SMEM)
    col = lambda off: pl.BlockSpec((1, ts, LANES), lambda b_, h, i: (b_, i, off + h))
    row_out = pl.BlockSpec((2, 1, ts, LANES), lambda b_, h, i: (0, b_, i, h))
    big = lambda dt: jax.ShapeDtypeStruct((2, b, s, GDN_W), dt)
    return pl.pallas_call(
        functools.partial(_gdn_wy_body, nchunk=nchunk),
        grid=(b, hh, s // ts),
        in_specs=[smem, smem, col(0), col(hh), col(2 * hh),
                  pl.BlockSpec((1, 1, ts, 4), lambda b_, h, i: (b_, h, i, 0)),
                  pl.BlockSpec((1, 1, 4, ts), lambda b_, h, i: (b_, h, 0, i))],
        out_specs=[row_out, row_out,
                   pl.BlockSpec((2, 1, 1, ts, GDN_CHUNK), lambda b_, h, i: (0, b_, h, i, 0)),
                   row_out, row_out,
                   pl.BlockSpec((2, 1, 1, nchunk, 1, LANES), lambda b_, h, i: (0, b_, h, i, 0, 0))],
        out_shape=[big(F32), big(BF16), jax.ShapeDtypeStruct((2, b, hh, s, GDN_CHUNK), BF16),
                   big(BF16), big(BF16), jax.ShapeDtypeStruct((2, b, hh, n, 1, LANES), F32)],
        compiler_params=_cparams("parallel", "parallel", "parallel"),
        name="gdn_wy",
    )(a_log.astype(F32), dt_bias.astype(F32), qkv, qkv, qkv, ba_col, ba_row)


def _gdn_scan_body(*refs, hb):
    ins = refs[:12]
    of_ref, ob_ref, s_ref = refs[12], refs[13], refs[14]
    o_refs = (of_ref, ob_ref)

    @pl.when(pl.program_id(2) == 0)
    def _():
        s_ref[...] = jnp.zeros_like(s_ref)

    for d in range(2):
        u_ref, w_ref, qk_ref, qg_ref, kg_ref, eg_ref = ins[6 * d:6 * d + 6]
        for hh in range(hb):
            cs = slice(hh * GDN_D, (hh + 1) * GDN_D)
            st = s_ref[d, hh]
            stb = st.astype(BF16)
            v_new = u_ref[0, 0, :, cs] - _dot(w_ref[0, 0, :, cs], stb)
            v_new_b = v_new.astype(BF16)
            o_refs[d][0, :, cs] = _dot(qg_ref[0, 0, :, cs], stb) + _dot(qk_ref[0, 0, hh], v_new_b)
            s_ref[d, hh] = st * eg_ref[0, 0, hh, 0] + _dot_tn(kg_ref[0, 0, :, cs], v_new_b)


def gdn_scan(u, w, qk, qg, kg, eg):
    _, b, s, _ = u.shape
    c_ = GDN_CHUNK
    n = s // c_
    hb = GDN_HEADS
    def dir_specs(d):
        ci = (lambda i: i) if d == 0 else (lambda i: n - 1 - i)
        row = pl.BlockSpec((1, 1, c_, hb * GDN_D), lambda b_, g, i: (d, b_, ci(i), g))
        return [row, row,
                pl.BlockSpec((1, 1, hb, c_, c_), lambda b_, g, i: (d, b_, g, ci(i), 0)),
                row, row,
                pl.BlockSpec((1, 1, hb, 1, 1, LANES), lambda b_, g, i: (d, b_, g, ci(i), 0, 0))]

    in_specs = dir_specs(0) + dir_specs(1)
    args = [u, w, qk, qg, kg, eg] * 2
    out_f = pl.BlockSpec((1, c_, hb * GDN_D), lambda b_, g, i: (b_, i, g))
    out_b = pl.BlockSpec((1, c_, hb * GDN_D), lambda b_, g, i: (b_, n - 1 - i, g))
    return pl.pallas_call(
        functools.partial(_gdn_scan_body, hb=hb),
        grid=(b, GDN_HEADS // hb, n),
        in_specs=in_specs,
        out_specs=[out_f, out_b],
        out_shape=[jax.ShapeDtypeStruct((b, s, GDN_W), F32)] * 2,
        scratch_shapes=[pltpu.VMEM((2, hb, GDN_D, GDN_D), F32)],
        compiler_params=_cparams("parallel", "parallel", "arbitrary"),
        name="gdn_scan",
    )(*args)


def _gdn_out_body(of_ref, ob_ref, z_ref, nw_ref, o_ref):
    o = of_ref[...] + ob_ref[...]
    z = z_ref[...]
    o = o * lax.rsqrt(jnp.mean(o * o, axis=-1, keepdims=True) + EPS) * nw_ref[...]
    o_ref[...] = (o * (z * jax.nn.sigmoid(z))).astype(o_ref.dtype)


def gdn_out(o_f, o_b, p2, norm_w):
    n, _ = o_f.shape
    ts = min(1024, n)
    blk = lambda off: pl.BlockSpec((ts, LANES), lambda i, h: (i, off + h))
    return pl.pallas_call(
        _gdn_out_body,
        grid=(n // ts, GDN_HEADS),
        in_specs=[blk(0), blk(0), blk(P_Z // LANES), pl.BlockSpec((1, LANES), lambda i, h: (0, 0))],
        out_specs=blk(0),
        out_shape=jax.ShapeDtypeStruct((n, GDN_W), BF16),
        compiler_params=_cparams("parallel", "parallel"),
        name="gdn_out",
    )(o_f, o_b, p2, norm_w.reshape(1, LANES).astype(F32))


def gdn_mixer(p3, conv_w, a_log, dt_bias, norm_w):
    b, s, _ = p3.shape
    qkv = gdn_prep(p3, conv_w)
    ba = p3[..., P_BA:P_BA + 4 * GDN_HEADS].reshape(b, s, 4, GDN_HEADS)
    ba_col = jnp.transpose(ba, (0, 3, 1, 2))
    ba_row = jnp.transpose(ba, (0, 3, 2, 1))
    u, w, qk, qg, kg, eg = gdn_wy(qkv, ba_col, ba_row, a_log, dt_bias)
    o_f, o_b = gdn_scan(u, w, qk, qg, kg, eg)
    return gdn_out(o_f.reshape(b * s, GDN_W), o_b.reshape(b * s, GDN_W), p3.reshape(b * s, P_COLS), norm_w)


def _angle_tables(rows, cols, period):
    k = lax.broadcasted_iota(jnp.int32, (rows, cols), 0)
    s = lax.broadcasted_iota(jnp.int32, (rows, cols), 1)
    ang = ((k * s) % period).astype(F32) * (2.0 * math.pi / period)
    return k, s, jnp.cos(ang), jnp.sin(ang)


def hyena_tables(s):
    k, t, c, sn = _angle_tables(s, s, 2 * s)
    alt = lambda idx: (1 - 2 * (idx % 2)).astype(F32)
    fwd = jnp.stack([c, jnp.where(k == 0, alt(t), sn)]).astype(BF16)
    inv = jnp.stack([c, jnp.where(t == 0, alt(k), sn)]).astype(BF16)
    return fwd, inv


def fnet_tables(s):
    _, _, c, sn = _angle_tables(s, s, s)
    return jnp.stack([c, sn]).astype(BF16)


def _seq_dft_body(*refs, nk, epilogue):
    t_ref, x_ref = refs[0], refs[1]
    acc_ref = refs[-1]
    o_ref = refs[-2]
    extra = refs[2:-2]
    kk = pl.program_id(2)

    @pl.when(kk == 0)
    def _():
        acc_ref[...] = jnp.zeros_like(acc_ref)

    xb = x_ref[0].astype(BF16)
    acc_ref[0] += _dot(t_ref[0], xb)
    acc_ref[1] += _dot(t_ref[1], xb)

    @pl.when(kk == nk - 1)
    def _():
        epilogue(acc_ref, extra, o_ref)


def _seq_dft(table, x, wd, col_blk, extra_args, extra_specs, epilogue, out_shape, out_spec, tf, tk, name):
    b, s, _ = x.shape
    tf = min(tf, s)
    tk = min(tk, s)
    nk = s // tk
    return pl.pallas_call(
        functools.partial(_seq_dft_body, nk=nk, epilogue=epilogue),
        grid=(b, s // tf, nk),
        in_specs=[pl.BlockSpec((2, tf, tk), lambda b_, f, kk: (0, f, kk)),
                  pl.BlockSpec((1, tk, wd), lambda b_, f, kk: (b_, kk, col_blk))] + extra_specs,
        out_specs=out_spec,
        out_shape=out_shape,
        scratch_shapes=[pltpu.VMEM((2, tf, wd), F32)],
        compiler_params=_cparams("parallel", "parallel", "arbitrary"),
        name=name,
    )(table, x, *extra_args)


def _hy_prep_body(*refs, ns):
    i = pl.program_id(1)
    w_ref, b_ref = refs[9], refs[10]
    vg_ref, x1_ref = refs[11], refs[12]
    outs = []
    for part in range(3):
        x_ref, xp_ref, xn_ref = refs[3 * part:3 * part + 3]
        outs.append(_conv3(x_ref[0], xp_ref[0], xn_ref[0], w_ref[part], i, ns) + b_ref[part])
    x1, x2, v = outs
    vg_ref[0] = v * x2
    x1_ref[0] = x1


def hy_prep(p3, conv_w, conv_b):
    b, s, _ = p3.shape
    ts = min(512, s)
    ns = s // ts
    ng = HY_W // LANES
    base = P_HY // LANES
    in_specs = []
    for part in range(3):
        in_specs += _halo_specs(ts, s, functools.partial(lambda part_, j: base + part_ * ng + j, part))
    in_specs += [pl.BlockSpec((3, 3, LANES), lambda b_, i, j: (0, 0, j)),
                 pl.BlockSpec((3, 1, LANES), lambda b_, i, j: (0, 0, j))]
    out = pl.BlockSpec((1, ts, LANES), lambda b_, i, j: (b_, i, j))
    cw = jnp.transpose(conv_w.astype(F32).reshape(3, 3, HY_W), (1, 0, 2))
    cb = conv_b.astype(F32).reshape(3, 1, HY_W)
    return pl.pallas_call(
        functools.partial(_hy_prep_body, ns=ns),
        grid=(b, ns, ng),
        in_specs=in_specs,
        out_specs=[out, out],
        out_shape=[jax.ShapeDtypeStruct((b, s, HY_W), F32)] * 2,
        compiler_params=_cparams("parallel", "parallel", "parallel"),
        name="hy_prep",
    )(*([p3] * 9), cw, cb)


def _hy_filter_body(z_ref, w1_ref, b1_ref, w2_ref, b2_ref, w3_ref, b3_ref, w4_ref, b4_ref, fr_ref, dl_ref,
                    buf_ref, sum_ref, *, s, tr):
    i = pl.program_id(0)
    z = z_ref[...]
    fr = fr_ref[...]
    hid = jnp.sin(fr * (_dot(z, w1_ref[...], HIGHEST) + b1_ref[...]))
    hid = jnp.sin(fr * (_dot(hid, w2_ref[...], HIGHEST) + b2_ref[...]))
    hid = jnp.sin(fr * (_dot(hid, w3_ref[...], HIGHEST) + b3_ref[...]))
    hh = _dot(hid, w4_ref[...], HIGHEST) + b4_ref[...]
    window = jnp.exp(-z[:, 0:1] * dl_ref[...])
    row = i * tr + lax.broadcasted_iota(jnp.int32, (tr, HY_W), 0)
    val = jnp.where(row < s, hh[:, :HY_W], hh[:, HY_W:]) * window
    val = jnp.where(row == s, 0.0, val)
    buf_ref[...] = val

    @pl.when(i == 0)
    def _():
        sum_ref[...] = jnp.zeros_like(sum_ref)

    sum_ref[...] += jnp.sum(jnp.abs(val), axis=0, keepdims=True)


def hy_filter(s, w1, b1, w2, b2, w3, b3, w4, b4, freq):
    n2 = 2 * s
    slot = jnp.arange(n2)
    pos = jnp.where(slot <= s, jnp.minimum(slot, s - 1), n2 - slot)
    t_all = jnp.linspace(0.0, 1.0, s, dtype=F32)
    ang_all = (2.0 * math.pi / s) * jnp.arange(s, dtype=F32)
    t = t_all[pos][:, None]
    ang = ang_all[pos][:, None]
    f = jnp.linspace(1e-4, HY_BANDS - 1, HY_BANDS, dtype=F32)[None, :]
    z = jnp.concatenate([t, jnp.cos(f * ang), -jnp.sin(f * ang)], axis=-1)
    z = jnp.pad(z, ((0, 0), (0, LANES - HY_EMB)))
    w1p = jnp.pad(w1.astype(F32), ((0, LANES - HY_EMB), (0, 0)))
    max_decay = math.log(1e-2) / 0.3
    min_decay = math.log(1e-2) / 1.5
    deltas = jnp.abs(jnp.linspace(min_decay, max_decay, HY_W, dtype=F32))[None, :]
    tr = min(512, n2)
    full = lambda a: pl.BlockSpec(a.shape, lambda i: (0,) * a.ndim)
    r2 = lambda a: a.astype(F32).reshape(1, -1)
    args = [z, w1p, r2(b1), w2.astype(F32), r2(b2), w3.astype(F32), r2(b3), w4.astype(F32), r2(b4), r2(freq), deltas]
    return pl.pallas_call(
        functools.partial(_hy_filter_body, s=s, tr=tr),
        grid=(n2 // tr,),
        in_specs=[pl.BlockSpec((tr, LANES), lambda i: (i, 0))] + [full(a) for a in args[1:]],
        out_specs=[pl.BlockSpec((tr, HY_W), lambda i: (i, 0)), pl.BlockSpec((1, HY_W), lambda i: (0, 0))],
        out_shape=[jax.ShapeDtypeStruct((n2, HY_W), F32), jax.ShapeDtypeStruct((1, HY_W), F32)],
        compiler_params=_cparams("arbitrary"),
        name="hy_filter",
    )(*args)


def _spec_raw_epilogue(acc_ref, extra, o_ref):
    o_ref[0] = acc_ref[...]


def _hy_spec_combine_body(a_ref, sum_ref, o_ref, *, tf):
    f = pl.program_id(0)
    k = f * tf + lax.broadcasted_iota(jnp.int32, (tf, HY_W), 0)
    sign = (1 - 2 * (k % 2)).astype(F32)
    inv = 1.0 / sum_ref[...]
    first = a_ref[0]
    second = a_ref[1]
    o_ref[0] = (first[0] + sign * second[0]) * inv
    sign1 = jnp.where(k == 0, 1.0, sign)
    o_ref[1] = (first[1] + sign1 * second[1]) * inv


def hy_filter_spectrum(buf, colsum, fwd_table, s):
    halves = buf.reshape(2, s, HY_W)
    tf = min(512, s)
    raw = _seq_dft(fwd_table, halves, HY_W, 0, [], [], _spec_raw_epilogue,
                   jax.ShapeDtypeStruct((2, 2, s, HY_W), F32),
                   pl.BlockSpec((1, 2, tf, HY_W), lambda b_, f, kk: (b_, 0, f, 0)), tf, 512, "hy_filter_dft")
    return pl.pallas_call(
        functools.partial(_hy_spec_combine_body, tf=tf),
        grid=(s // tf,),
        in_specs=[pl.BlockSpec((2, 2, tf, HY_W), lambda f: (0, 0, f, 0)), pl.BlockSpec((1, HY_W), lambda f: (0, 0))],
        out_specs=pl.BlockSpec((2, tf, HY_W), lambda f: (0, f, 0)),
        out_shape=jax.ShapeDtypeStruct((2, s, HY_W), F32),
        compiler_params=_cparams("parallel"),
        name="hy_filter_combine",
    )(raw, colsum)


def _hy_fwd_epilogue(acc_ref, extra, o_ref, *, s, tf):
    h_ref = extra[0]
    f = pl.program_id(1)
    xc, xs = acc_ref[0], acc_ref[1]
    hc, hs = h_ref[0], h_ref[1]
    row0 = (f * tf + lax.broadcasted_iota(jnp.int32, xc.shape, 0)) == 0
    scale = jnp.where(row0, 1.0 / (2 * s), 2.0 / (2 * s))
    yc = jnp.where(row0, xc * hc, xc * hc - xs * hs) * scale
    ys = jnp.where(row0, xs * hs, xc * hs + xs * hc) * scale
    o_ref[0, 0] = yc.astype(o_ref.dtype)
    o_ref[0, 1] = ys.astype(o_ref.dtype)


def _hy_inv_body(t_ref, y_ref, vg_ref, x1_ref, skip_ref, o_ref, acc_ref, *, nk):
    kk = pl.program_id(2)

    @pl.when(kk == 0)
    def _():
        acc_ref[...] = jnp.zeros_like(acc_ref)

    acc_ref[...] += _dot(t_ref[0], y_ref[0, 0]) + _dot(t_ref[1], y_ref[0, 1])

    @pl.when(kk == nk - 1)
    def _():
        o_ref[0] = ((acc_ref[...] + vg_ref[0] * skip_ref[...]) * x1_ref[0]).astype(o_ref.dtype)


def hyena_mixer(p3, conv_w, conv_b, spectrum, skip, fwd_table, inv_table):
    b, s, _ = p3.shape
    vg, x1 = hy_prep(p3, conv_w, conv_b)
    tf = min(512, s)
    tk = min(512, s)
    yp = _seq_dft(fwd_table, vg, HY_W, 0, [spectrum], [pl.BlockSpec((2, tf, HY_W), lambda b_, f, kk: (0, f, 0))],
                  functools.partial(_hy_fwd_epilogue, s=s, tf=tf),
                  jax.ShapeDtypeStruct((b, 2, s, HY_W), BF16),
                  pl.BlockSpec((1, 2, tf, HY_W), lambda b_, f, kk: (b_, 0, f, 0)), tf, tk, "hy_fwd_dft")
    nk = s // tk
    tile = pl.BlockSpec((1, tf, HY_W), lambda b_, f, kk: (b_, f, 0))
    return pl.pallas_call(
        functools.partial(_hy_inv_body, nk=nk),
        grid=(b, s // tf, nk),
        in_specs=[pl.BlockSpec((2, tf, tk), lambda b_, f, kk: (0, f, kk)),
                  pl.BlockSpec((1, 2, tk, HY_W), lambda b_, f, kk: (b_, 0, kk, 0)),
                  tile, tile, pl.BlockSpec((1, HY_W), lambda b_, f, kk: (0, 0))],
        out_specs=tile,
        out_shape=jax.ShapeDtypeStruct((b, s, HY_W), BF16),
        scratch_shapes=[pltpu.VMEM((tf, HY_W), F32)],
        compiler_params=_cparams("parallel", "parallel", "arbitrary"),
        name="hy_inv_dft",
    )(inv_table, yp, vg, x1, skip.astype(F32).reshape(1, HY_W))


def _fnet_epilogue(acc_ref, extra, o_ref):
    cw_ref = extra[0]
    for g in range(FN_GROUPS):
        cs = slice(g * FN_GC, (g + 1) * FN_GC)
        y = _dot(acc_ref[0, :, cs].astype(BF16), cw_ref[0, g]) + _dot(acc_ref[1, :, cs].astype(BF16), cw_ref[1, g])
        o_ref[0, :, cs] = y.astype(o_ref.dtype)


def _fnet_fold_body(t_ref, w_ref, o_ref, *, scale):
    for g in range(FN_GROUPS):
        o_ref[0, g] = (_dot(t_ref[0], w_ref[g], HIGHEST) * scale).astype(o_ref.dtype)
        o_ref[1, g] = (_dot(t_ref[1], w_ref[g], HIGHEST) * (-scale)).astype(o_ref.dtype)


def fnet_mixer(p3, w, table):
    b, s, _ = p3.shape
    _, _, cc, sc = _angle_tables(FN_GC, FN_GC, FN_GC)
    folded = pl.pallas_call(
        functools.partial(_fnet_fold_body, scale=1.0 / math.sqrt(s * FN_GC)),
        out_shape=jax.ShapeDtypeStruct((2, FN_GROUPS, FN_GC, FN_GC), BF16),
        name="fnet_fold",
    )(jnp.stack([cc, sc]), w.astype(F32))
    tf = min(512, s)
    return _seq_dft(table, p3, FN_W, P_FN // FN_W, [folded],
                    [pl.BlockSpec((2, FN_GROUPS, FN_GC, FN_GC), lambda b_, f, kk: (0, 0, 0, 0))],
                    _fnet_epilogue, jax.ShapeDtypeStruct((b, s, FN_W), BF16),
                    pl.BlockSpec((1, tf, FN_W), lambda b_, f, kk: (b_, f, 0)), tf, 512, "fnet_dft")


_CAND_GROUPS = [16, 8, 5, 4, 3, 2, 2, 2]
_CAND_ROWS = sum(-(-g // 8) * 8 for g in _CAND_GROUPS) + 8


def _extract_top(cur_list, val_refs, idx_refs, k):
    rows = [lax.broadcasted_iota(jnp.int32, c.shape, 0) for c in cur_list]

    def step(r, curs):
        out = []
        for t, cur in enumerate(curs):
            m = jnp.max(cur, axis=0, keepdims=True)
            first = jnp.min(jnp.where(cur == m, rows[t], cur.shape[0]), axis=0, keepdims=True)
            if val_refs[t] is not None:
                val_refs[t][pl.ds(r, 1), :] = m
            if idx_refs[t] is not None:
                idx_refs[t][pl.ds(r, 1), :] = first
            out.append(jnp.where(rows[t] == first, -jnp.inf, cur))
        return tuple(out)

    return lax.fori_loop(0, k, step, tuple(cur_list))


def _peer_score_body(q_ref, k1_ref, k2_ref, cnt_ref, e1_ref, rank2_ref, e2_ref, v_ref, idx_ref, sc_ref, c_ref):
    kk = PEER_TOPK
    tt = q_ref.shape[0]
    nlc = tt // LANES
    q = q_ref[...]
    s1 = _dot_nt(k1_ref[...], q[:, :PEER_HALF].astype(BF16))
    s2 = _dot_nt(k2_ref[...], q[:, PEER_HALF:].astype(BF16))
    for lc in range(nlc):
        ls = slice(lc * LANES, (lc + 1) * LANES)
        _extract_top([s1[:, ls], s2[:, ls]], [v_ref.at[0, lc], v_ref.at[1, lc]],
                     [idx_ref.at[0, lc], idx_ref.at[1, lc]], kk)
    rows8 = lax.broadcasted_iota(jnp.int32, (8, LANES), 0)
    cands = []
    for lc in range(nlc):
        v1 = v_ref[0, lc]
        v2 = v_ref[1, lc]
        pieces = []
        for i, g in enumerate(_CAND_GROUPS):
            gp = -(-g // 8) * 8
            piece = v1[i:i + 1, :] + v2[0:gp, :]
            pieces.append(piece if g == gp else jnp.where(rows8 < g, piece, -jnp.inf))
        pieces.append(v1[8:16, :] + v2[0:1, :])
        cands.append(jnp.concatenate(pieces, axis=0))
    finals = _extract_top(cands, [sc_ref.at[lc] for lc in range(nlc)], [None] * nlc, kk)
    rows = lax.broadcasted_iota(jnp.int32, (PEER_NKEYS, LANES), 0)
    for lc in range(nlc):
        ls = slice(lc * LANES, (lc + 1) * LANES)
        taken = jnp.where((finals[lc] == -jnp.inf) & (cands[lc] > -jnp.inf), 1.0, 0.0)
        off = 0
        for i, g in enumerate(_CAND_GROUPS):
            gp = -(-g // 8) * 8
            c_ref[i:i + 1, :] = jnp.sum(taken[off:off + gp], axis=0, keepdims=True)
            off += gp
        c_ref[8:16, :] = taken[off:off + 8]
        sc = sc_ref[lc]
        z = jnp.sum(jnp.exp(sc - sc[0:1, :]), axis=0, keepdims=True)
        cnt = jnp.zeros((PEER_NKEYS, LANES), F32)
        rank2 = jnp.full((PEER_NKEYS, LANES), float(kk), F32)
        for r in range(kk):
            cnt = jnp.where(rows == idx_ref[0, lc, r:r + 1, :], c_ref[r:r + 1, :], cnt)
            rank2 = jnp.where(rows == idx_ref[1, lc, r:r + 1, :], float(r), rank2)
        cnt_ref[0, :, ls] = cnt
        rank2_ref[0, :, ls] = rank2.astype(BF16)
        e1_ref[0, :, ls] = jnp.exp(s1[:, ls] - v_ref[0, lc, 0:1, :]) / z
        e2_ref[0, :, ls] = jnp.exp(s2[:, ls] - v_ref[1, lc, 0:1, :]).astype(BF16)


def peer_scores(q, k1, k2):
    n, _ = q.shape
    tt = min(512, n)
    nlc = tt // LANES
    kk = PEER_TOPK
    f32 = jax.ShapeDtypeStruct((PEER_HEADS, PEER_NKEYS, n), F32)
    b16 = jax.ShapeDtypeStruct((PEER_HEADS, PEER_NKEYS, n), BF16)
    blk = pl.BlockSpec((1, PEER_NKEYS, tt), lambda i, h: (h, 0, i))
    keys = pl.BlockSpec((PEER_NKEYS, PEER_HALF), lambda i, h: (0, 0))
    return pl.pallas_call(
        _peer_score_body,
        grid=(n // tt, PEER_HEADS),
        in_specs=[pl.BlockSpec((tt, 2 * PEER_HALF), lambda i, h: (i, h)), keys, keys],
        out_specs=[blk, blk, blk, blk],
        out_shape=[f32, f32, b16, b16],
        scratch_shapes=[pltpu.VMEM((2, nlc, kk, LANES), F32), pltpu.VMEM((2, nlc, kk, LANES), jnp.int32),
                        pltpu.VMEM((nlc, kk, LANES), F32), pltpu.VMEM((kk, LANES), F32)],
        compiler_params=_cparams("parallel", "parallel"),
        name="peer_scores",
    )(q, k1.astype(BF16), k2.astype(BF16))


BF16_ROWS = 16


def _peer_dense_body(h_ref, u_ref, vt_ref, cnt_ref, e1_ref, rank2_ref, e2_ref, o_ref, acc_ref, gl_ref, ga_ref,
                     *, ne, r):
    e = pl.program_id(1)
    tt = h_ref.shape[0]
    nblk = PEER_NKEYS // BF16_ROWS

    @pl.when(e == 0)
    def _():
        acc_ref[...] = jnp.zeros_like(acc_ref)

    act = _dot_nt(u_ref[...], h_ref[...])
    gl_ref[...] = (0.5 * act * (1.0 + lax.erf(act * (1.0 / math.sqrt(2.0))))).astype(BF16)
    for rr in range(r):
        e1 = e * r + rr
        cnt_rows = [cnt_ref[hd, pl.ds(e1, 1), :] for hd in range(PEER_HEADS)]
        e1_rows = [e1_ref[hd, pl.ds(e1, 1), :] for hd in range(PEER_HEADS)]
        for lc in range(tt // LANES):
            ls = slice(lc * LANES, (lc + 1) * LANES)
            gate = [None] * nblk
            for hd in range(PEER_HEADS):
                cnt16 = jnp.broadcast_to(cnt_rows[hd][:, ls], (BF16_ROWS, LANES)).astype(BF16)
                w16 = jnp.broadcast_to(e1_rows[hd][:, ls], (BF16_ROWS, LANES)).astype(BF16)
                for kb in range(nblk):
                    ks = slice(kb * BF16_ROWS, (kb + 1) * BF16_ROWS)
                    term = jnp.where(rank2_ref[hd, ks, ls] < cnt16, e2_ref[hd, ks, ls], 0.0) * w16
                    gate[kb] = term if gate[kb] is None else gate[kb] + term
            for kb in range(nblk):
                es = slice(rr * PEER_NKEYS + kb * BF16_ROWS, rr * PEER_NKEYS + (kb + 1) * BF16_ROWS)
                ga_ref[es, ls] = gl_ref[es, ls] * gate[kb]
    acc_ref[...] += _dot(vt_ref[...], ga_ref[...])

    @pl.when(e == ne - 1)
    def _():
        o_ref[...] = acc_ref[...].T


def peer_dense(h, u_bf, vt_bf, cnt, e1, rank2, e2):
    n, d = h.shape
    ne_total = u_bf.shape[0]
    tt = min(512, n)
    r = 8
    te = r * PEER_NKEYS
    ne = ne_total // te
    tok = pl.BlockSpec((PEER_HEADS, PEER_NKEYS, tt), lambda i, e: (0, 0, i))
    return pl.pallas_call(
        functools.partial(_peer_dense_body, ne=ne, r=r),
        grid=(n // tt, ne),
        in_specs=[pl.BlockSpec((tt, d), lambda i, e: (i, 0)),
                  pl.BlockSpec((te, d), lambda i, e: (e, 0)),
                  pl.BlockSpec((d, te), lambda i, e: (0, e)),
                  tok, tok, tok, tok],
        out_specs=pl.BlockSpec((tt, d), lambda i, e: (i, 0)),
        out_shape=jax.ShapeDtypeStruct((n, d), F32),
        scratch_shapes=[pltpu.VMEM((d, tt), F32), pltpu.VMEM((te, tt), BF16), pltpu.VMEM((te, tt), BF16)],
        compiler_params=_cparams("parallel", "arbitrary"),
        name="peer_dense",
    )(h, u_bf, vt_bf, cnt, e1, rank2, e2)


def peer_ffn(x2d, norm_w, wq, k1, k2, u_tab, v_tab):
    h = rmsnorm(x2d, norm_w, BF16)
    q = linear([(h, wq.astype(BF16))], None, F32)
    cnt, e1, rank2, e2 = peer_scores(q, k1, k2)
    return peer_dense(h, u_tab.astype(BF16), v_tab.astype(BF16).T, cnt, e1, rank2, e2)


def _reorder_w_in(w):
    off_z, off_ba = 3 * GDN_W, 4 * GDN_W
    off_hy = off_ba + 4 * GDN_HEADS
    off_fn = off_hy + 3 * HY_W
    pad = jnp.zeros((w.shape[0], LANES - 4 * GDN_HEADS), w.dtype)
    return jnp.concatenate([w[:, :off_z], w[:, off_z:off_ba], w[:, off_hy:off_fn], w[:, off_fn:],
                            w[:, off_ba:off_hy], pad], axis=1)


def kernel(x, norm1_w, w_in, gdn_conv_w, gdn_a_log, gdn_dt_bias, gdn_norm_w, hy_conv_w, hy_conv_b, hy_w1, hy_b1, hy_w2, hy_b2, hy_w3, hy_b3, hy_w4, hy_b4, hy_freq, hy_skip, fnet_w, w_out, norm2_w, peer_wq, peer_k1, peer_k2, peer_u, peer_v, final_norm_w):
    b, s, d = x.shape
    n = b * s
    depth = w_in.shape[0]
    hy_fwd, hy_inv = hyena_tables(s)
    fn_tab = fnet_tables(s)
    x2d = x.reshape(n, d)
    delta = None
    for l in range(depth):
        if delta is None:
            h = rmsnorm(x2d, norm1_w[l], BF16)
        else:
            x2d, h = add_rmsnorm(x2d, delta, norm1_w[l], BF16, True)
        p = linear([(h, _reorder_w_in(w_in[l]).astype(BF16))], None, F32, tm=512, tn=896)
        p3 = p.reshape(b, s, P_COLS)
        y_a = gdn_mixer(p3, gdn_conv_w[l], gdn_a_log[l], gdn_dt_bias[l], gdn_norm_w[l])
        buf, colsum = hy_filter(s, hy_w1[l], hy_b1[l], hy_w2[l], hy_b2[l], hy_w3[l], hy_b3[l], hy_w4[l], hy_b4[l],
                                hy_freq[l])
        spectrum = hy_filter_spectrum(buf, colsum, hy_fwd, s)
        y_b = hyena_mixer(p3, hy_conv_w[l], hy_conv_b[l], spectrum, hy_skip[l], hy_fwd, hy_inv)
        y_c = fnet_mixer(p3, fnet_w[l], fn_tab)
        wo = w_out[l].astype(BF16)
        x2d = linear([(y_a, wo[:GDN_W]), (y_b.reshape(n, HY_W), wo[GDN_W:GDN_W + HY_W]),
                      (y_c.reshape(n, FN_W), wo[GDN_W + HY_W:])], x2d, F32)
        delta = peer_ffn(x2d, norm2_w[l], peer_wq[l], peer_k1[l], peer_k2[l], peer_u[l], peer_v[l])
    return add_rmsnorm(x2d, delta, final_norm_w, F32, False).reshape(b, s, d)
```

```python
import functools
import math

import jax
import jax.numpy as jnp
from jax import lax
from jax.experimental import pallas as pl
from jax.experimental.pallas import tpu as pltpu

F32 = jnp.float32
BF16 = jnp.bfloat16
EPS = 1e-6
LANES = 128
VMEM_LIMIT_BYTES = 56 * 1024 * 1024
HIGHEST = lax.Precision.HIGHEST

GDN_HEADS = 8
GDN_D = 128
GDN_W = GDN_HEADS * GDN_D
GDN_CHUNK = 64
HY_W = 512
HY_EMB = 33
HY_BANDS = 16
HY_FFN = 64
FN_W = 512
FN_GROUPS = 4
FN_GC = 128
PEER_HEADS = 8
PEER_NKEYS = 128
PEER_HALF = 128
PEER_TOPK = 16
P_QKV = 0
P_Z = 3 * GDN_W
P_HY = P_Z + GDN_W
P_FN = P_HY + 3 * HY_W
P_BA = P_FN + FN_W
P_COLS = P_BA + LANES


def _cparams(*sem):
    return pltpu.CompilerParams(dimension_semantics=sem, vmem_limit_bytes=VMEM_LIMIT_BYTES)


def _dot(a, b, precision=None):
    return jnp.dot(a, b, preferred_element_type=F32, precision=precision)


def _dot_nt(a, b, precision=None):
    return lax.dot_general(a, b, (((1,), (1,)), ((), ())), preferred_element_type=F32, precision=precision)


def _dot_tn(a, b, precision=None):
    return lax.dot_general(a, b, (((0,), (0,)), ((), ())), preferred_element_type=F32, precision=precision)


def _rms_body(x_ref, w_ref, o_ref):
    x = x_ref[...]
    ms = jnp.mean(x * x, axis=-1, keepdims=True)
    o_ref[...] = (x * lax.rsqrt(ms + EPS) * w_ref[...]).astype(o_ref.dtype)


def rmsnorm(x2d, w, out_dtype):
    n, d = x2d.shape
    tm = min(256, n)
    return pl.pallas_call(
        _rms_body,
        grid=(n // tm,),
        in_specs=[pl.BlockSpec((tm, d), lambda i: (i, 0)), pl.BlockSpec((1, d), lambda i: (0, 0))],
        out_specs=pl.BlockSpec((tm, d), lambda i: (i, 0)),
        out_shape=jax.ShapeDtypeStruct((n, d), out_dtype),
        compiler_params=_cparams("parallel"),
        name="rmsnorm",
    )(x2d, w.reshape(1, d).astype(F32))


def _add_rms_body(x_ref, y_ref, w_ref, *o_refs):
    x = x_ref[...] + y_ref[...]
    ms = jnp.mean(x * x, axis=-1, keepdims=True)
    o_refs[-1][...] = (x * lax.rsqrt(ms + EPS) * w_ref[...]).astype(o_refs[-1].dtype)
    if len(o_refs) == 2:
        o_refs[0][...] = x


def add_rmsnorm(x2d, y2d, w, out_dtype, emit_sum):
    n, d = x2d.shape
    tm = min(256, n)
    row = pl.BlockSpec((tm, d), lambda i: (i, 0))
    normed = jax.ShapeDtypeStruct((n, d), out_dtype)
    return pl.pallas_call(
        _add_rms_body,
        grid=(n // tm,),
        in_specs=[row, row, pl.BlockSpec((1, d), lambda i: (0, 0))],
        out_specs=[row, row] if emit_sum else row,
        out_shape=[jax.ShapeDtypeStruct((n, d), F32), normed] if emit_sum else normed,
        compiler_params=_cparams("parallel"),
        name="add_rmsnorm",
    )(x2d, y2d, w.reshape(1, d).astype(F32))


def _linear_body(*refs, npairs, has_res):
    o_ref = refs[-1]
    acc = _dot(refs[0][...], refs[1][...])
    for p in range(1, npairs):
        acc = acc + _dot(refs[2 * p][...], refs[2 * p + 1][...])
    if has_res:
        acc = acc + refs[2 * npairs][...]
    o_ref[...] = acc.astype(o_ref.dtype)


def linear(pairs, res, out_dtype, tm=1024, tn=512):
    m = pairs[0][0].shape[0]
    n = pairs[0][1].shape[1]
    tm = min(tm, m)
    tn = min(tn, n)
    assert m % tm == 0 and n % tn == 0
    in_specs, args = [], []
    for a, b in pairs:
        k = a.shape[1]
        in_specs += [pl.BlockSpec((tm, k), lambda j, i: (i, 0)), pl.BlockSpec((k, tn), lambda j, i: (0, j))]
        args += [a, b]
    if res is not None:
        in_specs.append(pl.BlockSpec((tm, tn), lambda j, i: (i, j)))
        args.append(res)
    return pl.pallas_call(
        functools.partial(_linear_body, npairs=len(pairs), has_res=res is not None),
        grid=(n // tn, m // tm),
        in_specs=in_specs,
        out_specs=pl.BlockSpec((tm, tn), lambda j, i: (i, j)),
        out_shape=jax.ShapeDtypeStruct((m, n), out_dtype),
        compiler_params=_cparams("parallel", "parallel"),
        name="linear",
    )(*args)


def _conv3(x, prev_blk, next_blk, w, i, ns):
    ts = x.shape[0]
    prev_row = jnp.where(i > 0, prev_blk[7:8, :], 0.0)
    next_row = jnp.where(i < ns - 1, next_blk[0:1, :], 0.0)
    rows = lax.broadcasted_iota(jnp.int32, x.shape, 0)
    xm = jnp.where(rows == 0, prev_row, pltpu.roll(x, 1, 0))
    xp = jnp.where(rows == ts - 1, next_row, pltpu.roll(x, ts - 1, 0))
    return xm * w[0:1, :] + x * w[1:2, :] + xp * w[2:3, :]


def _halo_specs(ts, s, col_fn):
    nb8 = s // 8
    r = ts // 8
    cur = pl.BlockSpec((1, ts, LANES), lambda b, i, j: (b, i, col_fn(j)))
    prev = pl.BlockSpec((1, 8, LANES), lambda b, i, j: (b, jnp.maximum(i * r - 1, 0), col_fn(j)))
    nxt = pl.BlockSpec((1, 8, LANES), lambda b, i, j: (b, jnp.minimum((i + 1) * r, nb8 - 1), col_fn(j)))
    return [cur, prev, nxt]


def _gdn_wy_body(alog_ref, dtb_ref, *refs, nchunk, ns):
    xq, xk, xv = refs[0:3], refs[3:6], refs[6:9]
    wq_ref, wk_ref, wv_ref, bac_ref, bar_ref = refs[9:14]
    u_ref, w_ref, qk_ref, qg_ref, kg_ref, eg_ref = refs[14:20]
    q_ref, k_ref, v_ref = refs[20:23]
    c_ = GDN_CHUNK
    h = pl.program_id(1)
    i = pl.program_id(2)

    def prep(x, cw_ref, scale):
        y = _conv3(x[0][0], x[1][0], x[2][0], cw_ref[...], i, ns)
        y = y * jax.nn.sigmoid(y)
        if scale is None:
            return y
        return y * (lax.rsqrt(jnp.sum(y * y, axis=-1, keepdims=True) + EPS) * scale)

    q_ref[0] = prep(xq, wq_ref, GDN_D ** -0.5)
    k_ref[0] = prep(xk, wk_ref, 1.0)
    v_ref[0] = prep(xv, wv_ref, None)
    ii = lax.broadcasted_iota(jnp.int32, (c_, c_), 0)
    jj = lax.broadcasted_iota(jnp.int32, (c_, c_), 1)
    eye = (ii == jj).astype(F32)
    bdot = lambda a, b: _dot(a.astype(BF16), b.astype(BF16))
    gram = []
    for c in range(nchunk):
        sl = slice(c * c_, (c + 1) * c_)
        kb16 = k_ref[0, sl, :].astype(BF16)
        qk16 = jnp.concatenate([q_ref[0, sl, :].astype(BF16), kb16], axis=0)
        gram.append(_dot_nt(qk16, kb16))
    jobs = []
    for d in range(2):
        neg_a = -jnp.exp(jnp.full((1, 1), alog_ref[d, h], F32))
        dtb = dtb_ref[d, h]
        incl = (ii >= jj) if d == 0 else (ii <= jj)
        incl_t = (ii <= jj) if d == 0 else (ii >= jj)
        strict = (ii > jj) if d == 0 else (ii < jj)
        for c in range(nchunk):
            sl = slice(c * c_, (c + 1) * c_)
            beta = jax.nn.sigmoid(bac_ref[0, 0, sl, d:d + 1])
            lg_c = neg_a * jax.nn.softplus(bac_ref[0, 0, sl, 2 + d:3 + d] + dtb)
            lg_r = neg_a * jax.nn.softplus(bar_ref[0, 0, 2 + d:3 + d, sl] + dtb)
            g_c = jnp.sum(jnp.where(incl, lg_r, 0.0), axis=1, keepdims=True)
            g_r = jnp.sum(jnp.where(incl_t, lg_c, 0.0), axis=0, keepdims=True)
            g_tot = jnp.sum(lg_r, axis=1, keepdims=True)
            decay = jnp.where(incl, jnp.exp(jnp.where(incl, g_c - g_r, 0.0)), 0.0)
            qk_ref[d, 0, 0, sl, :] = jnp.where(incl, gram[c][:c_] * decay, 0.0).astype(qk_ref.dtype)
            eg_ref[d, 0, 0, c] = jnp.broadcast_to(jnp.exp(g_tot), (1, LANES))
            p = jnp.where(strict, gram[c][c_:] * decay, 0.0) * (-beta)
            jobs.append(dict(d=d, sl=sl, beta=beta, g_c=g_c, g_tot=g_tot, p=p, x=eye + p))
    for _ in range(int(math.log2(c_)) - 1):
        for jb in jobs:
            jb["p"] = bdot(jb["p"], jb["p"])
        for jb in jobs:
            jb["x"] = jb["x"] + bdot(jb["x"], jb["p"])
    for jb in jobs:
        d, sl, beta, g_c = jb["d"], jb["sl"], jb["beta"], jb["g_c"]
        k = k_ref[0, sl, :]
        kb = k * beta
        rhs = jnp.concatenate([v_ref[0, sl, :] * beta, kb * jnp.exp(g_c)], axis=1)
        sol = bdot(jb["x"], rhs)
        u_ref[d, 0, sl, :] = sol[:, :GDN_D]
        w_ref[d, 0, sl, :] = sol[:, GDN_D:].astype(w_ref.dtype)
        qg_ref[d, 0, sl, :] = (q_ref[0, sl, :] * jnp.exp(g_c)).astype(qg_ref.dtype)
        kg_ref[d, 0, sl, :] = (k * jnp.exp(jb["g_tot"] - g_c)).astype(kg_ref.dtype)


def gdn_wy(p3, conv_w, ba_col, ba_row, a_log, dt_bias):
    b, s, _ = p3.shape
    hh = GDN_HEADS
    ts = min(512, s)
    nchunk = ts // GDN_CHUNK
    n = s // GDN_CHUNK
    nb8 = s // 8
    r8 = ts // 8
    smem = pl.BlockSpec(memory_space=pltpu.SMEM)

    def halo(off):
        return [pl.BlockSpec((1, ts, LANES), lambda b_, h, i: (b_, i, off + h)),
                pl.BlockSpec((1, 8, LANES), lambda b_, h, i: (b_, jnp.maximum(i * r8 - 1, 0), off + h)),
                pl.BlockSpec((1, 8, LANES), lambda b_, h, i: (b_, jnp.minimum((i + 1) * r8, nb8 - 1), off + h))]

    cw = lambda off: pl.BlockSpec((3, LANES), lambda b_, h, i: (0, off + h))
    row_out = pl.BlockSpec((2, 1, ts, LANES), lambda b_, h, i: (0, b_, i, h))
    big = lambda dt: jax.ShapeDtypeStruct((2, b, s, GDN_W), dt)
    conv_w = conv_w.astype(F32)
    return pl.pallas_call(
        functools.partial(_gdn_wy_body, nchunk=nchunk, ns=s // ts),
        grid=(b, hh, s // ts),
        in_specs=[smem, smem] + halo(0) + halo(hh) + halo(2 * hh) + [cw(0), cw(hh), cw(2 * hh),
                  pl.BlockSpec((1, 1, ts, 4), lambda b_, h, i: (b_, h, i, 0)),
                  pl.BlockSpec((1, 1, 4, ts), lambda b_, h, i: (b_, h, 0, i))],
        scratch_shapes=[pltpu.VMEM((1, ts, LANES), F32)] * 3,
        out_specs=[row_out, row_out,
                   pl.BlockSpec((2, 1, 1, ts, GDN_CHUNK), lambda b_, h, i: (0, b_, h, i, 0)),
                   row_out, row_out,
                   pl.BlockSpec((2, 1, 1, nchunk, 1, LANES), lambda b_, h, i: (0, b_, h, i, 0, 0))],
        out_shape=[big(F32), big(BF16), jax.ShapeDtypeStruct((2, b, hh, s, GDN_CHUNK), BF16),
                   big(BF16), big(BF16), jax.ShapeDtypeStruct((2, b, hh, n, 1, LANES), F32)],
        compiler_params=_cparams("parallel", "parallel", "parallel"),
        name="gdn_wy",
    )(a_log.astype(F32), dt_bias.astype(F32), *([p3] * 9), conv_w, conv_w, conv_w, ba_col, ba_row)


def _gdn_scan_body(*refs, hb):
    ins = refs[:12]
    of_ref, ob_ref, s_ref = refs[12], refs[13], refs[14]
    o_refs = (of_ref, ob_ref)

    @pl.when(pl.program_id(2) == 0)
    def _():
        s_ref[...] = jnp.zeros_like(s_ref)

    chains = [(d, hh) for d in range(2) for hh in range(hb)]
    cols = lambda hh: slice(hh * GDN_D, (hh + 1) * GDN_D)
    st = {c: s_ref[c[0], c[1]] for c in chains}
    stb = {c: st[c].astype(BF16) for c in chains}
    ws = {(d, hh): _dot(ins[6 * d + 1][0, 0, :, cols(hh)], stb[(d, hh)]) for d, hh in chains}
    qs = {(d, hh): _dot(ins[6 * d + 3][0, 0, :, cols(hh)], stb[(d, hh)]) for d, hh in chains}
    v_new = {(d, hh): (ins[6 * d][0, 0, :, cols(hh)] - ws[(d, hh)]).astype(BF16) for d, hh in chains}
    for d, hh in chains:
        o_refs[d][0, :, cols(hh)] = qs[(d, hh)] + _dot(ins[6 * d + 2][0, 0, hh], v_new[(d, hh)])
    for d, hh in chains:
        s_ref[d, hh] = (st[(d, hh)] * ins[6 * d + 5][0, 0, hh, 0]
                        + _dot_tn(ins[6 * d + 4][0, 0, :, cols(hh)], v_new[(d, hh)]))


def gdn_scan(u, w, qk, qg, kg, eg):
    _, b, s, _ = u.shape
    c_ = GDN_CHUNK
    n = s // c_
    hb = GDN_HEADS
    def dir_specs(d):
        ci = (lambda i: i) if d == 0 else (lambda i: n - 1 - i)
        row = pl.BlockSpec((1, 1, c_, hb * GDN_D), lambda b_, g, i: (d, b_, ci(i), g))
        return [row, row,
                pl.BlockSpec((1, 1, hb, c_, c_), lambda b_, g, i: (d, b_, g, ci(i), 0)),
                row, row,
                pl.BlockSpec((1, 1, hb, 1, 1, LANES), lambda b_, g, i: (d, b_, g, ci(i), 0, 0))]

    in_specs = dir_specs(0) + dir_specs(1)
    args = [u, w, qk, qg, kg, eg] * 2
    out_f = pl.BlockSpec((1, c_, hb * GDN_D), lambda b_, g, i: (b_, i, g))
    out_b = pl.BlockSpec((1, c_, hb * GDN_D), lambda b_, g, i: (b_, n - 1 - i, g))
    return pl.pallas_call(
        functools.partial(_gdn_scan_body, hb=hb),
        grid=(b, GDN_HEADS // hb, n),
        in_specs=in_specs,
        out_specs=[out_f, out_b],
        out_shape=[jax.ShapeDtypeStruct((b, s, GDN_W), F32)] * 2,
        scratch_shapes=[pltpu.VMEM((2, hb, GDN_D, GDN_D), F32)],
        compiler_params=_cparams("parallel", "parallel", "arbitrary"),
        name="gdn_scan",
    )(*args)


def _gdn_out_body(of_ref, ob_ref, z_ref, nw_ref, o_ref):
    o = of_ref[...] + ob_ref[...]
    z = z_ref[...]
    o = o * lax.rsqrt(jnp.mean(o * o, axis=-1, keepdims=True) + EPS) * nw_ref[...]
    o_ref[...] = (o * (z * jax.nn.sigmoid(z))).astype(o_ref.dtype)


def gdn_out(o_f, o_b, p2, norm_w):
    n, _ = o_f.shape
    ts = min(1024, n)
    blk = lambda off: pl.BlockSpec((ts, LANES), lambda i, h: (i, off + h))
    return pl.pallas_call(
        _gdn_out_body,
        grid=(n // ts, GDN_HEADS),
        in_specs=[blk(0), blk(0), blk(P_Z // LANES), pl.BlockSpec((1, LANES), lambda i, h: (0, 0))],
        out_specs=blk(0),
        out_shape=jax.ShapeDtypeStruct((n, GDN_W), BF16),
        compiler_params=_cparams("parallel", "parallel"),
        name="gdn_out",
    )(o_f, o_b, p2, norm_w.reshape(1, LANES).astype(F32))


def gdn_mixer(p3, conv_w, a_log, dt_bias, norm_w):
    b, s, _ = p3.shape
    ba = p3[..., P_BA:P_BA + 4 * GDN_HEADS].reshape(b, s, 4, GDN_HEADS)
    ba_col = jnp.transpose(ba, (0, 3, 1, 2))
    ba_row = jnp.transpose(ba, (0, 3, 2, 1))
    u, w, qk, qg, kg, eg = gdn_wy(p3, conv_w, ba_col, ba_row, a_log, dt_bias)
    o_f, o_b = gdn_scan(u, w, qk, qg, kg, eg)
    return gdn_out(o_f.reshape(b * s, GDN_W), o_b.reshape(b * s, GDN_W), p3.reshape(b * s, P_COLS), norm_w)


DFT_ROWS = 2048
TABLE_SPLIT = 64


def _angle_tables(rows, cols, period):
    k = lax.broadcasted_iota(jnp.int32, (rows, cols), 0)
    s = lax.broadcasted_iota(jnp.int32, (rows, cols), 1)
    if rows % TABLE_SPLIT or rows <= TABLE_SPLIT:
        ang = ((k * s) % period).astype(F32) * (2.0 * math.pi / period)
        return k, s, jnp.cos(ang), jnp.sin(ang)
    nhi = rows // TABLE_SPLIT
    _, _, ca, sa = _angle_tables_scaled(nhi, cols, period, TABLE_SPLIT)
    _, _, cb, sb = _angle_tables_scaled(TABLE_SPLIT, cols, period, 1)
    c = (ca[:, None, :] * cb[None, :, :] - sa[:, None, :] * sb[None, :, :]).reshape(rows, cols)
    sn = (sa[:, None, :] * cb[None, :, :] + ca[:, None, :] * sb[None, :, :]).reshape(rows, cols)
    return k, s, c, sn


def _angle_tables_scaled(rows, cols, period, mult):
    k = lax.broadcasted_iota(jnp.int32, (rows, cols), 0) * mult
    s = lax.broadcasted_iota(jnp.int32, (rows, cols), 1)
    ang = ((k * s) % period).astype(F32) * (2.0 * math.pi / period)
    return k, s, jnp.cos(ang), jnp.sin(ang)


def hyena_tables(s):
    k, t, c, sn = _angle_tables(s, s, 2 * s)
    alt = lambda idx: (1 - 2 * (idx % 2)).astype(F32)
    fwd = jnp.stack([c, jnp.where(k == 0, alt(t), sn)]).astype(BF16)
    inv = jnp.stack([c, jnp.where(t == 0, alt(k), sn)]).astype(BF16)
    return fwd, inv


def fnet_tables(s):
    _, _, c, sn = _angle_tables(s, s, s)
    return jnp.stack([c, sn]).astype(BF16)


def _seq_dft_body(*refs, nk, epilogue):
    t_ref, x_ref = refs[0], refs[1]
    acc_ref = refs[-1]
    o_ref = refs[-2]
    extra = refs[2:-2]
    kk = pl.program_id(2)

    @pl.when(kk == 0)
    def _():
        acc_ref[...] = jnp.zeros_like(acc_ref)

    xb = x_ref[0].astype(BF16)
    acc_ref[0] += _dot(t_ref[0], xb)
    acc_ref[1] += _dot(t_ref[1], xb)

    @pl.when(kk == nk - 1)
    def _():
        epilogue(acc_ref, extra, o_ref)


def _seq_dft(table, x, wd, col_blk, extra_args, extra_specs, epilogue, out_shape, out_spec, tf, tk, name):
    b, s, _ = x.shape
    tf = min(tf, s)
    tk = min(tk, s)
    nk = s // tk
    return pl.pallas_call(
        functools.partial(_seq_dft_body, nk=nk, epilogue=epilogue),
        grid=(b, s // tf, nk),
        in_specs=[pl.BlockSpec((2, tf, tk), lambda b_, f, kk: (0, f, kk)),
                  pl.BlockSpec((1, tk, wd), lambda b_, f, kk: (b_, kk, col_blk))] + extra_specs,
        out_specs=out_spec,
        out_shape=out_shape,
        scratch_shapes=[pltpu.VMEM((2, tf, wd), F32)],
        compiler_params=_cparams("parallel", "parallel", "arbitrary"),
        name=name,
    )(table, x, *extra_args)


def _hy_prep_body(*refs, ns):
    i = pl.program_id(1)
    w_ref, b_ref = refs[9], refs[10]
    vg_ref, x1_ref = refs[11], refs[12]
    outs = []
    for part in range(3):
        x_ref, xp_ref, xn_ref = refs[3 * part:3 * part + 3]
        outs.append(_conv3(x_ref[0], xp_ref[0], xn_ref[0], w_ref[part], i, ns) + b_ref[part])
    x1, x2, v = outs
    vg_ref[0] = v * x2
    x1_ref[0] = x1


def hy_prep(p3, conv_w, conv_b):
    b, s, _ = p3.shape
    ts = min(512, s)
    ns = s // ts
    ng = HY_W // LANES
    base = P_HY // LANES
    in_specs = []
    for part in range(3):
        in_specs += _halo_specs(ts, s, functools.partial(lambda part_, j: base + part_ * ng + j, part))
    in_specs += [pl.BlockSpec((3, 3, LANES), lambda b_, i, j: (0, 0, j)),
                 pl.BlockSpec((3, 1, LANES), lambda b_, i, j: (0, 0, j))]
    out = pl.BlockSpec((1, ts, LANES), lambda b_, i, j: (b_, i, j))
    cw = jnp.transpose(conv_w.astype(F32).reshape(3, 3, HY_W), (1, 0, 2))
    cb = conv_b.astype(F32).reshape(3, 1, HY_W)
    return pl.pallas_call(
        functools.partial(_hy_prep_body, ns=ns),
        grid=(b, ns, ng),
        in_specs=in_specs,
        out_specs=[out, out],
        out_shape=[jax.ShapeDtypeStruct((b, s, HY_W), F32)] * 2,
        compiler_params=_cparams("parallel", "parallel", "parallel"),
        name="hy_prep",
    )(*([p3] * 9), cw, cb)


def _hy_filter_body(z_ref, w1_ref, b1_ref, w2_ref, b2_ref, w3_ref, b3_ref, w4_ref, b4_ref, fr_ref, dl_ref,
                    buf_ref, sum_ref, *, s, tr):
    i = pl.program_id(0)
    z = z_ref[...]
    fr = fr_ref[...]
    hid = jnp.sin(fr * (_dot(z, w1_ref[...], HIGHEST) + b1_ref[...]))
    hid = jnp.sin(fr * (_dot(hid, w2_ref[...], HIGHEST) + b2_ref[...]))
    hid = jnp.sin(fr * (_dot(hid, w3_ref[...], HIGHEST) + b3_ref[...]))
    hh = _dot(hid, w4_ref[...], HIGHEST) + b4_ref[...]
    window = jnp.exp(-z[:, 0:1] * dl_ref[...])
    row = i * tr + lax.broadcasted_iota(jnp.int32, (tr, HY_W), 0)
    val = jnp.where(row < s, hh[:, :HY_W], hh[:, HY_W:]) * window
    val = jnp.where(row == s, 0.0, val)
    buf_ref[...] = val

    @pl.when(i == 0)
    def _():
        sum_ref[...] = jnp.zeros_like(sum_ref)

    sum_ref[...] += jnp.sum(jnp.abs(val), axis=0, keepdims=True)


def hy_filter(s, w1, b1, w2, b2, w3, b3, w4, b4, freq):
    n2 = 2 * s
    slot = jnp.arange(n2)
    pos = jnp.where(slot <= s, jnp.minimum(slot, s - 1), n2 - slot)
    t_all = jnp.linspace(0.0, 1.0, s, dtype=F32)
    ang_all = (2.0 * math.pi / s) * jnp.arange(s, dtype=F32)
    t = t_all[pos][:, None]
    ang = ang_all[pos][:, None]
    f = jnp.linspace(1e-4, HY_BANDS - 1, HY_BANDS, dtype=F32)[None, :]
    z = jnp.concatenate([t, jnp.cos(f * ang), -jnp.sin(f * ang)], axis=-1)
    z = jnp.pad(z, ((0, 0), (0, LANES - HY_EMB)))
    w1p = jnp.pad(w1.astype(F32), ((0, LANES - HY_EMB), (0, 0)))
    max_decay = math.log(1e-2) / 0.3
    min_decay = math.log(1e-2) / 1.5
    deltas = jnp.abs(jnp.linspace(min_decay, max_decay, HY_W, dtype=F32))[None, :]
    tr = min(512, n2)
    full = lambda a: pl.BlockSpec(a.shape, lambda i: (0,) * a.ndim)
    r2 = lambda a: a.astype(F32).reshape(1, -1)
    args = [z, w1p, r2(b1), w2.astype(F32), r2(b2), w3.astype(F32), r2(b3), w4.astype(F32), r2(b4), r2(freq), deltas]
    return pl.pallas_call(
        functools.partial(_hy_filter_body, s=s, tr=tr),
        grid=(n2 // tr,),
        in_specs=[pl.BlockSpec((tr, LANES), lambda i: (i, 0))] + [full(a) for a in args[1:]],
        out_specs=[pl.BlockSpec((tr, HY_W), lambda i: (i, 0)), pl.BlockSpec((1, HY_W), lambda i: (0, 0))],
        out_shape=[jax.ShapeDtypeStruct((n2, HY_W), F32), jax.ShapeDtypeStruct((1, HY_W), F32)],
        compiler_params=_cparams("arbitrary"),
        name="hy_filter",
    )(*args)


def _spec_raw_epilogue(acc_ref, extra, o_ref):
    o_ref[0] = acc_ref[...]


def _hy_spec_combine_body(a_ref, sum_ref, o_ref, *, tf):
    f = pl.program_id(0)
    k = f * tf + lax.broadcasted_iota(jnp.int32, (tf, HY_W), 0)
    sign = (1 - 2 * (k % 2)).astype(F32)
    inv = 1.0 / sum_ref[...]
    first = a_ref[0]
    second = a_ref[1]
    o_ref[0] = (first[0] + sign * second[0]) * inv
    sign1 = jnp.where(k == 0, 1.0, sign)
    o_ref[1] = (first[1] + sign1 * second[1]) * inv


def hy_filter_spectrum(buf, colsum, fwd_table, s):
    halves = buf.reshape(2, s, HY_W)
    tf = min(DFT_ROWS, s)
    raw = _seq_dft(fwd_table, halves, HY_W, 0, [], [], _spec_raw_epilogue,
                   jax.ShapeDtypeStruct((2, 2, s, HY_W), F32),
                   pl.BlockSpec((1, 2, tf, HY_W), lambda b_, f, kk: (b_, 0, f, 0)), tf, 512, "hy_filter_dft")
    tf = min(512, s)
    return pl.pallas_call(
        functools.partial(_hy_spec_combine_body, tf=tf),
        grid=(s // tf,),
        in_specs=[pl.BlockSpec((2, 2, tf, HY_W), lambda f: (0, 0, f, 0)), pl.BlockSpec((1, HY_W), lambda f: (0, 0))],
        out_specs=pl.BlockSpec((2, tf, HY_W), lambda f: (0, f, 0)),
        out_shape=jax.ShapeDtypeStruct((2, s, HY_W), F32),
        compiler_params=_cparams("parallel"),
        name="hy_filter_combine",
    )(raw, colsum)


def _hy_fwd_epilogue(acc_ref, extra, o_ref, *, s, tf):
    h_ref = extra[0]
    f = pl.program_id(1)
    xc, xs = acc_ref[0], acc_ref[1]
    hc, hs = h_ref[0], h_ref[1]
    row0 = (f * tf + lax.broadcasted_iota(jnp.int32, xc.shape, 0)) == 0
    scale = jnp.where(row0, 1.0 / (2 * s), 2.0 / (2 * s))
    yc = jnp.where(row0, xc * hc, xc * hc - xs * hs) * scale
    ys = jnp.where(row0, xs * hs, xc * hs + xs * hc) * scale
    o_ref[0, 0] = yc.astype(o_ref.dtype)
    o_ref[0, 1] = ys.astype(o_ref.dtype)


def _hy_inv_body(t_ref, y_ref, vg_ref, x1_ref, skip_ref, o_ref, acc_ref, *, nk):
    kk = pl.program_id(2)

    @pl.when(kk == 0)
    def _():
        acc_ref[...] = jnp.zeros_like(acc_ref)

    acc_ref[...] += _dot(t_ref[0], y_ref[0, 0]) + _dot(t_ref[1], y_ref[0, 1])

    @pl.when(kk == nk - 1)
    def _():
        o_ref[0] = ((acc_ref[...] + vg_ref[0] * skip_ref[...]) * x1_ref[0]).astype(o_ref.dtype)


def hyena_mixer(p3, conv_w, conv_b, spectrum, skip, fwd_table, inv_table):
    b, s, _ = p3.shape
    vg, x1 = hy_prep(p3, conv_w, conv_b)
    tf = min(DFT_ROWS, s)
    tk = min(512, s)
    yp = _seq_dft(fwd_table, vg, HY_W, 0, [spectrum], [pl.BlockSpec((2, tf, HY_W), lambda b_, f, kk: (0, f, 0))],
                  functools.partial(_hy_fwd_epilogue, s=s, tf=tf),
                  jax.ShapeDtypeStruct((b, 2, s, HY_W), BF16),
                  pl.BlockSpec((1, 2, tf, HY_W), lambda b_, f, kk: (b_, 0, f, 0)), tf, tk, "hy_fwd_dft")
    nk = s // tk
    tile = pl.BlockSpec((1, tf, HY_W), lambda b_, f, kk: (b_, f, 0))
    return pl.pallas_call(
        functools.partial(_hy_inv_body, nk=nk),
        grid=(b, s // tf, nk),
        in_specs=[pl.BlockSpec((2, tf, tk), lambda b_, f, kk: (0, f, kk)),
                  pl.BlockSpec((1, 2, tk, HY_W), lambda b_, f, kk: (b_, 0, kk, 0)),
                  tile, tile, pl.BlockSpec((1, HY_W), lambda b_, f, kk: (0, 0))],
        out_specs=tile,
        out_shape=jax.ShapeDtypeStruct((b, s, HY_W), BF16),
        scratch_shapes=[pltpu.VMEM((tf, HY_W), F32)],
        compiler_params=_cparams("parallel", "parallel", "arbitrary"),
        name="hy_inv_dft",
    )(inv_table, yp, vg, x1, skip.astype(F32).reshape(1, HY_W))


def _fnet_epilogue(acc_ref, extra, o_ref):
    cw_ref = extra[0]
    for g in range(FN_GROUPS):
        cs = slice(g * FN_GC, (g + 1) * FN_GC)
        y = _dot(acc_ref[0, :, cs].astype(BF16), cw_ref[0, g]) + _dot(acc_ref[1, :, cs].astype(BF16), cw_ref[1, g])
        o_ref[0, :, cs] = y.astype(o_ref.dtype)


def _fnet_fold_body(t_ref, w_ref, o_ref, *, scale):
    for g in range(FN_GROUPS):
        o_ref[0, g] = (_dot(t_ref[0], w_ref[g], HIGHEST) * scale).astype(o_ref.dtype)
        o_ref[1, g] = (_dot(t_ref[1], w_ref[g], HIGHEST) * (-scale)).astype(o_ref.dtype)


def fnet_mixer(p3, w, table):
    b, s, _ = p3.shape
    _, _, cc, sc = _angle_tables(FN_GC, FN_GC, FN_GC)
    folded = pl.pallas_call(
        functools.partial(_fnet_fold_body, scale=1.0 / math.sqrt(s * FN_GC)),
        out_shape=jax.ShapeDtypeStruct((2, FN_GROUPS, FN_GC, FN_GC), BF16),
        name="fnet_fold",
    )(jnp.stack([cc, sc]), w.astype(F32))
    tf = min(DFT_ROWS, s)
    return _seq_dft(table, p3, FN_W, P_FN // FN_W, [folded],
                    [pl.BlockSpec((2, FN_GROUPS, FN_GC, FN_GC), lambda b_, f, kk: (0, 0, 0, 0))],
                    _fnet_epilogue, jax.ShapeDtypeStruct((b, s, FN_W), BF16),
                    pl.BlockSpec((1, tf, FN_W), lambda b_, f, kk: (b_, f, 0)), tf, 512, "fnet_dft")


_CAND_GROUPS = [16, 8, 5, 4, 3, 2, 2, 2]
_CAND_ROWS = sum(-(-g // 8) * 8 for g in _CAND_GROUPS) + 8


def _extract_top(cur_list, val_refs, idx_refs, k):
    rows = [lax.broadcasted_iota(jnp.int32, c.shape, 0) for c in cur_list]

    def step(r, curs):
        out = []
        for t, cur in enumerate(curs):
            m = jnp.max(cur, axis=0, keepdims=True)
            first = jnp.min(jnp.where(cur == m, rows[t], cur.shape[0]), axis=0, keepdims=True)
            if val_refs[t] is not None:
                val_refs[t][pl.ds(r, 1), :] = m
            if idx_refs[t] is not None:
                idx_refs[t][pl.ds(r, 1), :] = first
            out.append(jnp.where(rows[t] == first, -jnp.inf, cur))
        return tuple(out)

    return lax.fori_loop(0, k, step, tuple(cur_list))


def _bf16_pair_bits(x):
    bits = pltpu.bitcast(x, jnp.uint32)
    rounded = bits + jnp.uint32(0x7FFF) + ((bits >> 16) & jnp.uint32(1))
    hi = rounded & jnp.uint32(0xFFFF0000)
    return hi | (hi >> 16)


def _peer_score_body(q_ref, k1_ref, k2_ref, cnt_ref, e1_ref, rank2_ref, e2_ref, v_ref, idx_ref, sc_ref, c_ref):
    kk = PEER_TOPK
    tt = q_ref.shape[0]
    nlc = tt // LANES
    q = q_ref[...]
    s1 = _dot_nt(k1_ref[...], q[:, :PEER_HALF].astype(BF16))
    s2 = _dot_nt(k2_ref[...], q[:, PEER_HALF:].astype(BF16))
    for lc in range(nlc):
        ls = slice(lc * LANES, (lc + 1) * LANES)
        _extract_top([s1[:, ls], s2[:, ls]], [v_ref.at[0, lc], v_ref.at[1, lc]],
                     [idx_ref.at[0, lc], idx_ref.at[1, lc]], kk)
    rows8 = lax.broadcasted_iota(jnp.int32, (8, LANES), 0)
    cands = []
    for lc in range(nlc):
        v1 = v_ref[0, lc]
        v2 = v_ref[1, lc]
        pieces = []
        for i, g in enumerate(_CAND_GROUPS):
            gp = -(-g // 8) * 8
            piece = v1[i:i + 1, :] + v2[0:gp, :]
            pieces.append(piece if g == gp else jnp.where(rows8 < g, piece, -jnp.inf))
        pieces.append(v1[8:16, :] + v2[0:1, :])
        cands.append(jnp.concatenate(pieces, axis=0))
    finals = _extract_top(cands, [sc_ref.at[lc] for lc in range(nlc)], [None] * nlc, kk)
    rows = lax.broadcasted_iota(jnp.int32, (PEER_NKEYS, LANES), 0)
    for lc in range(nlc):
        ls = slice(lc * LANES, (lc + 1) * LANES)
        taken = jnp.where((finals[lc] == -jnp.inf) & (cands[lc] > -jnp.inf), 1.0, 0.0)
        off = 0
        for i, g in enumerate(_CAND_GROUPS):
            gp = -(-g // 8) * 8
            c_ref[i:i + 1, :] = jnp.sum(taken[off:off + gp], axis=0, keepdims=True)
            off += gp
        c_ref[8:16, :] = taken[off:off + 8]
        sc = sc_ref[lc]
        z = jnp.sum(jnp.exp(sc - sc[0:1, :]), axis=0, keepdims=True)
        cnt = jnp.zeros((PEER_NKEYS, LANES), F32)
        rank2 = jnp.full((PEER_NKEYS, LANES), float(kk), F32)
        for r in range(kk):
            cnt = jnp.where(rows == idx_ref[0, lc, r:r + 1, :], c_ref[r:r + 1, :], cnt)
            rank2 = jnp.where(rows == idx_ref[1, lc, r:r + 1, :], float(r), rank2)
        cnt_ref[0, :, ls] = _bf16_pair_bits(cnt)
        rank2_ref[0, :, ls] = rank2.astype(BF16)
        e1_ref[0, :, ls] = _bf16_pair_bits(jnp.exp(s1[:, ls] - v_ref[0, lc, 0:1, :]) / z)
        e2_ref[0, :, ls] = jnp.exp(s2[:, ls] - v_ref[1, lc, 0:1, :]).astype(BF16)


def peer_scores(q, k1, k2):
    n, _ = q.shape
    tt = min(512, n)
    nlc = tt // LANES
    kk = PEER_TOPK
    rows = jax.ShapeDtypeStruct((PEER_HEADS, PEER_NKEYS, n), jnp.uint32)
    b16 = jax.ShapeDtypeStruct((PEER_HEADS, PEER_NKEYS, n), BF16)
    blk = pl.BlockSpec((1, PEER_NKEYS, tt), lambda i, h: (h, 0, i))
    keys = pl.BlockSpec((PEER_NKEYS, PEER_HALF), lambda i, h: (0, 0))
    return pl.pallas_call(
        _peer_score_body,
        grid=(n // tt, PEER_HEADS),
        in_specs=[pl.BlockSpec((tt, 2 * PEER_HALF), lambda i, h: (i, h)), keys, keys],
        out_specs=[blk, blk, blk, blk],
        out_shape=[rows, rows, b16, b16],
        scratch_shapes=[pltpu.VMEM((2, nlc, kk, LANES), F32), pltpu.VMEM((2, nlc, kk, LANES), jnp.int32),
                        pltpu.VMEM((nlc, kk, LANES), F32), pltpu.VMEM((kk, LANES), F32)],
        compiler_params=_cparams("parallel", "parallel"),
        name="peer_scores",
    )(q, k1.astype(BF16), k2.astype(BF16))


BF16_ROWS = 16


def _peer_dense_body(h_ref, u_ref, vt_ref, cnt_ref, e1_ref, rank2_in, e2_in, o_ref, acc_ref, gl_ref, ga_ref,
                     rank2_ref, e2_ref, *, ne, r):
    e = pl.program_id(1)
    tt = h_ref.shape[0]
    nblk = PEER_NKEYS // BF16_ROWS

    @pl.when(e == 0)
    def _():
        acc_ref[...] = jnp.zeros_like(acc_ref)
        rank2_ref[...] = rank2_in[...]
        e2_ref[...] = e2_in[...]

    act = _dot_nt(u_ref[...], h_ref[...])
    gl_ref[...] = (0.5 * act * (1.0 + lax.erf(act * (1.0 / math.sqrt(2.0))))).astype(BF16)
    for rr in range(r):
        e1 = e * r + rr
        cnt_rows = [cnt_ref[hd, pl.ds(e1, 1), :] for hd in range(PEER_HEADS)]
        e1_rows = [e1_ref[hd, pl.ds(e1, 1), :] for hd in range(PEER_HEADS)]
        for lc in range(tt // LANES):
            ls = slice(lc * LANES, (lc + 1) * LANES)
            gate = [None] * nblk
            for hd in range(PEER_HEADS):
                cnt16 = pltpu.bitcast(jnp.broadcast_to(cnt_rows[hd][:, ls], (BF16_ROWS // 2, LANES)), BF16)
                w16 = pltpu.bitcast(jnp.broadcast_to(e1_rows[hd][:, ls], (BF16_ROWS // 2, LANES)), BF16)
                for kb in range(nblk):
                    ks = slice(kb * BF16_ROWS, (kb + 1) * BF16_ROWS)
                    term = jnp.where(rank2_ref[hd, ks, ls] < cnt16, e2_ref[hd, ks, ls], 0.0) * w16
                    gate[kb] = term if gate[kb] is None else gate[kb] + term
            for kb in range(nblk):
                es = slice(rr * PEER_NKEYS + kb * BF16_ROWS, rr * PEER_NKEYS + (kb + 1) * BF16_ROWS)
                ga_ref[es, ls] = gl_ref[es, ls] * gate[kb]
    acc_ref[...] += _dot(vt_ref[...], ga_ref[...])

    @pl.when(e == ne - 1)
    def _():
        o_ref[...] = acc_ref[...].T


def peer_dense(h, u_bf, vt_bf, cnt, e1, rank2, e2):
    n, d = h.shape
    ne_total = u_bf.shape[0]
    tt = min(512, n)
    r = 8
    te = r * PEER_NKEYS
    ne = ne_total // te
    tok = pl.BlockSpec((PEER_HEADS, PEER_NKEYS, tt), lambda i, e: (0, 0, i))
    return pl.pallas_call(
        functools.partial(_peer_dense_body, ne=ne, r=r),
        grid=(n // tt, ne),
        in_specs=[pl.BlockSpec((tt, d), lambda i, e: (i, 0)),
                  pl.BlockSpec((te, d), lambda i, e: (e, 0)),
                  pl.BlockSpec((d, te), lambda i, e: (0, e)),
                  tok, tok, tok, tok],
        out_specs=pl.BlockSpec((tt, d), lambda i, e: (i, 0)),
        out_shape=jax.ShapeDtypeStruct((n, d), F32),
        scratch_shapes=[pltpu.VMEM((d, tt), F32), pltpu.VMEM((te, tt), BF16), pltpu.VMEM((te, tt), BF16),
                        pltpu.VMEM((PEER_HEADS, PEER_NKEYS, tt), BF16), pltpu.VMEM((PEER_HEADS, PEER_NKEYS, tt), BF16)],
        compiler_params=_cparams("parallel", "arbitrary"),
        name="peer_dense",
    )(h, u_bf, vt_bf, cnt, e1, rank2, e2)


def peer_ffn(x2d, norm_w, wq, k1, k2, u_tab, v_tab):
    h = rmsnorm(x2d, norm_w, BF16)
    q = linear([(h, wq.astype(BF16))], None, F32)
    cnt, e1, rank2, e2 = peer_scores(q, k1, k2)
    return peer_dense(h, u_tab.astype(BF16), v_tab.astype(BF16).T, cnt, e1, rank2, e2)


def _reorder_w_in(w):
    off_z, off_ba = 3 * GDN_W, 4 * GDN_W
    off_hy = off_ba + 4 * GDN_HEADS
    off_fn = off_hy + 3 * HY_W
    pad = jnp.zeros((w.shape[0], LANES - 4 * GDN_HEADS), w.dtype)
    return jnp.concatenate([w[:, :off_z], w[:, off_z:off_ba], w[:, off_hy:off_fn], w[:, off_fn:],
                            w[:, off_ba:off_hy], pad], axis=1)


def kernel(x, norm1_w, w_in, gdn_conv_w, gdn_a_log, gdn_dt_bias, gdn_norm_w, hy_conv_w, hy_conv_b, hy_w1, hy_b1, hy_w2, hy_b2, hy_w3, hy_b3, hy_w4, hy_b4, hy_freq, hy_skip, fnet_w, w_out, norm2_w, peer_wq, peer_k1, peer_k2, peer_u, peer_v, final_norm_w):
    b, s, d = x.shape
    n = b * s
    depth = w_in.shape[0]
    hy_fwd, hy_inv = hyena_tables(s)
    fn_tab = fnet_tables(s)
    x2d = x.reshape(n, d)
    delta = None
    for l in range(depth):
        if delta is None:
            h = rmsnorm(x2d, norm1_w[l], BF16)
        else:
            x2d, h = add_rmsnorm(x2d, delta, norm1_w[l], BF16, True)
        p = linear([(h, _reorder_w_in(w_in[l]).astype(BF16))], None, F32, tm=1024, tn=896)
        p3 = p.reshape(b, s, P_COLS)
        y_a = gdn_mixer(p3, gdn_conv_w[l], gdn_a_log[l], gdn_dt_bias[l], gdn_norm_w[l])
        buf, colsum = hy_filter(s, hy_w1[l], hy_b1[l], hy_w2[l], hy_b2[l], hy_w3[l], hy_b3[l], hy_w4[l], hy_b4[l],
                                hy_freq[l])
        spectrum = hy_filter_spectrum(buf, colsum, hy_fwd, s)
        y_b = hyena_mixer(p3, hy_conv_w[l], hy_conv_b[l], spectrum, hy_skip[l], hy_fwd, hy_inv)
        y_c = fnet_mixer(p3, fnet_w[l], fn_tab)
        wo = w_out[l].astype(BF16)
        x2d = linear([(y_a, wo[:GDN_W]), (y_b.reshape(n, HY_W), wo[GDN_W:GDN_W + HY_W]),
                      (y_c.reshape(n, FN_W), wo[GDN_W + HY_W:])], x2d, F32)
        delta = peer_ffn(x2d, norm2_w[l], peer_wq[l], peer_k1[l], peer_k2[l], peer_u[l], peer_v[l])
    return add_rmsnorm(x2d, delta, final_norm_w, F32, False).reshape(b, s, d)
```

```python
import functools
import math

import jax
import jax.numpy as jnp
from jax import lax
from jax.experimental import pallas as pl
from jax.experimental.pallas import tpu as pltpu

F32 = jnp.float32
BF16 = jnp.bfloat16
EPS = 1e-6
LANES = 128
VMEM_LIMIT_BYTES = 56 * 1024 * 1024
HIGHEST = lax.Precision.HIGHEST

GDN_HEADS = 8
GDN_D = 128
GDN_W = GDN_HEADS * GDN_D
GDN_CHUNK = 64
HY_W = 512
HY_EMB = 33
HY_BANDS = 16
HY_FFN = 64
FN_W = 512
FN_GROUPS = 4
FN_GC = 128
PEER_HEADS = 8
PEER_NKEYS = 128
PEER_HALF = 128
PEER_TOPK = 16
P_QKV = 0
P_Z = 3 * GDN_W
P_HY = P_Z + GDN_W
P_FN = P_HY + 3 * HY_W
P_BA = P_FN + FN_W
P_COLS = P_BA + LANES


def _cparams(*sem):
    return pltpu.CompilerParams(dimension_semantics=sem, vmem_limit_bytes=VMEM_LIMIT_BYTES)


def _dot(a, b, precision=None):
    return jnp.dot(a, b, preferred_element_type=F32, precision=precision)


def _dot_nt(a, b, precision=None):
    return lax.dot_general(a, b, (((1,), (1,)), ((), ())), preferred_element_type=F32, precision=precision)


def _dot_tn(a, b, precision=None):
    return lax.dot_general(a, b, (((0,), (0,)), ((), ())), preferred_element_type=F32, precision=precision)


def _rms_body(x_ref, w_ref, o_ref):
    x = x_ref[...]
    ms = jnp.mean(x * x, axis=-1, keepdims=True)
    o_ref[...] = (x * lax.rsqrt(ms + EPS) * w_ref[...]).astype(o_ref.dtype)


def rmsnorm(x2d, w, out_dtype):
    n, d = x2d.shape
    tm = min(256, n)
    return pl.pallas_call(
        _rms_body,
        grid=(n // tm,),
        in_specs=[pl.BlockSpec((tm, d), lambda i: (i, 0)), pl.BlockSpec((1, d), lambda i: (0, 0))],
        out_specs=pl.BlockSpec((tm, d), lambda i: (i, 0)),
        out_shape=jax.ShapeDtypeStruct((n, d), out_dtype),
        compiler_params=_cparams("parallel"),
        name="rmsnorm",
    )(x2d, w.reshape(1, d).astype(F32))


def _add_rms_body(x_ref, y_ref, w_ref, *o_refs):
    x = x_ref[...] + y_ref[...]
    ms = jnp.mean(x * x, axis=-1, keepdims=True)
    o_refs[-1][...] = (x * lax.rsqrt(ms + EPS) * w_ref[...]).astype(o_refs[-1].dtype)
    if len(o_refs) == 2:
        o_refs[0][...] = x


def add_rmsnorm(x2d, y2d, w, out_dtype, emit_sum):
    n, d = x2d.shape
    tm = min(256, n)
    row = pl.BlockSpec((tm, d), lambda i: (i, 0))
    normed = jax.ShapeDtypeStruct((n, d), out_dtype)
    return pl.pallas_call(
        _add_rms_body,
        grid=(n // tm,),
        in_specs=[row, row, pl.BlockSpec((1, d), lambda i: (0, 0))],
        out_specs=[row, row] if emit_sum else row,
        out_shape=[jax.ShapeDtypeStruct((n, d), F32), normed] if emit_sum else normed,
        compiler_params=_cparams("parallel"),
        name="add_rmsnorm",
    )(x2d, y2d, w.reshape(1, d).astype(F32))


def _linear_body(*refs, npairs, has_res):
    o_ref = refs[-1]
    acc = _dot(refs[0][...], refs[1][...])
    for p in range(1, npairs):
        acc = acc + _dot(refs[2 * p][...], refs[2 * p + 1][...])
    if has_res:
        acc = acc + refs[2 * npairs][...]
    o_ref[...] = acc.astype(o_ref.dtype)


def linear(pairs, res, out_dtype, tm=1024, tn=512):
    m = pairs[0][0].shape[0]
    n = pairs[0][1].shape[1]
    tm = min(tm, m)
    tn = min(tn, n)
    assert m % tm == 0 and n % tn == 0
    in_specs, args = [], []
    for a, b in pairs:
        k = a.shape[1]
        in_specs += [pl.BlockSpec((tm, k), lambda j, i: (i, 0)), pl.BlockSpec((k, tn), lambda j, i: (0, j))]
        args += [a, b]
    if res is not None:
        in_specs.append(pl.BlockSpec((tm, tn), lambda j, i: (i, j)))
        args.append(res)
    return pl.pallas_call(
        functools.partial(_linear_body, npairs=len(pairs), has_res=res is not None),
        grid=(n // tn, m // tm),
        in_specs=in_specs,
        out_specs=pl.BlockSpec((tm, tn), lambda j, i: (i, j)),
        out_shape=jax.ShapeDtypeStruct((m, n), out_dtype),
        compiler_params=_cparams("parallel", "parallel"),
        name="linear",
    )(*args)


def _conv3(x, prev_blk, next_blk, w, i, ns):
    ts = x.shape[0]
    prev_row = jnp.where(i > 0, prev_blk[7:8, :], 0.0)
    next_row = jnp.where(i < ns - 1, next_blk[0:1, :], 0.0)
    rows = lax.broadcasted_iota(jnp.int32, x.shape, 0)
    xm = jnp.where(rows == 0, prev_row, pltpu.roll(x, 1, 0))
    xp = jnp.where(rows == ts - 1, next_row, pltpu.roll(x, ts - 1, 0))
    return xm * w[0:1, :] + x * w[1:2, :] + xp * w[2:3, :]


def _halo_specs(ts, s, col_fn):
    nb8 = s // 8
    r = ts // 8
    cur = pl.BlockSpec((1, ts, LANES), lambda b, i, j: (b, i, col_fn(j)))
    prev = pl.BlockSpec((1, 8, LANES), lambda b, i, j: (b, jnp.maximum(i * r - 1, 0), col_fn(j)))
    nxt = pl.BlockSpec((1, 8, LANES), lambda b, i, j: (b, jnp.minimum((i + 1) * r, nb8 - 1), col_fn(j)))
    return [cur, prev, nxt]


def _gdn_wy_body(alog_ref, dtb_ref, *refs, nchunk, ns):
    xq, xk, xv = refs[0:3], refs[3:6], refs[6:9]
    wq_ref, wk_ref, wv_ref, bac_ref, bar_ref = refs[9:14]
    u_ref, w_ref, qk_ref, qg_ref, kg_ref, eg_ref = refs[14:20]
    q_ref, k_ref, v_ref = refs[20:23]
    c_ = GDN_CHUNK
    h = pl.program_id(1)
    i = pl.program_id(2)

    def prep(x, cw_ref, scale):
        y = _conv3(x[0][0], x[1][0], x[2][0], cw_ref[...], i, ns)
        y = y * jax.nn.sigmoid(y)
        if scale is None:
            return y
        return y * (lax.rsqrt(jnp.sum(y * y, axis=-1, keepdims=True) + EPS) * scale)

    q_ref[0] = prep(xq, wq_ref, GDN_D ** -0.5)
    k_ref[0] = prep(xk, wk_ref, 1.0)
    v_ref[0] = prep(xv, wv_ref, None)
    ii = lax.broadcasted_iota(jnp.int32, (c_, c_), 0)
    jj = lax.broadcasted_iota(jnp.int32, (c_, c_), 1)
    eye = (ii == jj).astype(F32)
    bdot = lambda a, b: _dot(a.astype(BF16), b.astype(BF16))
    gram = []
    for c in range(nchunk):
        sl = slice(c * c_, (c + 1) * c_)
        kb16 = k_ref[0, sl, :].astype(BF16)
        qk16 = jnp.concatenate([q_ref[0, sl, :].astype(BF16), kb16], axis=0)
        gram.append(_dot_nt(qk16, kb16))
    jobs = []
    for d in range(2):
        neg_a = -jnp.exp(jnp.full((1, 1), alog_ref[d, h], F32))
        dtb = dtb_ref[d, h]
        incl = (ii >= jj) if d == 0 else (ii <= jj)
        incl_t = (ii <= jj) if d == 0 else (ii >= jj)
        strict = (ii > jj) if d == 0 else (ii < jj)
        for c in range(nchunk):
            sl = slice(c * c_, (c + 1) * c_)
            beta = jax.nn.sigmoid(bac_ref[0, 0, sl, d:d + 1])
            lg_c = neg_a * jax.nn.softplus(bac_ref[0, 0, sl, 2 + d:3 + d] + dtb)
            lg_r = neg_a * jax.nn.softplus(bar_ref[0, 0, 2 + d:3 + d, sl] + dtb)
            g_c = jnp.sum(jnp.where(incl, lg_r, 0.0), axis=1, keepdims=True)
            g_r = jnp.sum(jnp.where(incl_t, lg_c, 0.0), axis=0, keepdims=True)
            g_tot = jnp.sum(lg_r, axis=1, keepdims=True)
            decay = jnp.where(incl, jnp.exp(jnp.where(incl, g_c - g_r, 0.0)), 0.0)
            qk_ref[d, 0, 0, sl, :] = jnp.where(incl, gram[c][:c_] * decay, 0.0).astype(qk_ref.dtype)
            eg_ref[d, 0, 0, c] = jnp.broadcast_to(jnp.exp(g_tot), (1, LANES))
            p = jnp.where(strict, gram[c][c_:] * decay, 0.0) * (-beta)
            jobs.append(dict(d=d, sl=sl, beta=beta, g_c=g_c, g_tot=g_tot, p=p, x=eye + p))
    for _ in range(int(math.log2(c_)) - 1):
        for jb in jobs:
            jb["p"] = bdot(jb["p"], jb["p"])
        for jb in jobs:
            jb["x"] = jb["x"] + bdot(jb["x"], jb["p"])
    for jb in jobs:
        d, sl, beta, g_c = jb["d"], jb["sl"], jb["beta"], jb["g_c"]
        k = k_ref[0, sl, :]
        kb = k * beta
        rhs = jnp.concatenate([v_ref[0, sl, :] * beta, kb * jnp.exp(g_c)], axis=1)
        sol = bdot(jb["x"], rhs)
        u_ref[d, 0, sl, :] = sol[:, :GDN_D]
        w_ref[d, 0, sl, :] = sol[:, GDN_D:].astype(w_ref.dtype)
        qg_ref[d, 0, sl, :] = (q_ref[0, sl, :] * jnp.exp(g_c)).astype(qg_ref.dtype)
        kg_ref[d, 0, sl, :] = (k * jnp.exp(jb["g_tot"] - g_c)).astype(kg_ref.dtype)


def gdn_wy(p3, conv_w, ba_col, ba_row, a_log, dt_bias):
    b, s, _ = p3.shape
    hh = GDN_HEADS
    ts = min(512, s)
    nchunk = ts // GDN_CHUNK
    n = s // GDN_CHUNK
    nb8 = s // 8
    r8 = ts // 8
    smem = pl.BlockSpec(memory_space=pltpu.SMEM)

    def halo(off):
        return [pl.BlockSpec((1, ts, LANES), lambda b_, h, i: (b_, i, off + h)),
                pl.BlockSpec((1, 8, LANES), lambda b_, h, i: (b_, jnp.maximum(i * r8 - 1, 0), off + h)),
                pl.BlockSpec((1, 8, LANES), lambda b_, h, i: (b_, jnp.minimum((i + 1) * r8, nb8 - 1), off + h))]

    cw = lambda off: pl.BlockSpec((3, LANES), lambda b_, h, i: (0, off + h))
    row_out = pl.BlockSpec((2, 1, ts, LANES), lambda b_, h, i: (0, b_, i, h))
    big = lambda dt: jax.ShapeDtypeStruct((2, b, s, GDN_W), dt)
    conv_w = conv_w.astype(F32)
    return pl.pallas_call(
        functools.partial(_gdn_wy_body, nchunk=nchunk, ns=s // ts),
        grid=(b, hh, s // ts),
        in_specs=[smem, smem] + halo(0) + halo(hh) + halo(2 * hh) + [cw(0), cw(hh), cw(2 * hh),
                  pl.BlockSpec((1, 1, ts, 4), lambda b_, h, i: (b_, h, i, 0)),
                  pl.BlockSpec((1, 1, 4, ts), lambda b_, h, i: (b_, h, 0, i))],
        scratch_shapes=[pltpu.VMEM((1, ts, LANES), F32)] * 3,
        out_specs=[row_out, row_out,
                   pl.BlockSpec((2, 1, 1, ts, GDN_CHUNK), lambda b_, h, i: (0, b_, h, i, 0)),
                   row_out, row_out,
                   pl.BlockSpec((2, 1, 1, nchunk, 1, LANES), lambda b_, h, i: (0, b_, h, i, 0, 0))],
        out_shape=[big(F32), big(BF16), jax.ShapeDtypeStruct((2, b, hh, s, GDN_CHUNK), BF16),
                   big(BF16), big(BF16), jax.ShapeDtypeStruct((2, b, hh, n, 1, LANES), F32)],
        compiler_params=_cparams("parallel", "parallel", "parallel"),
        name="gdn_wy",
    )(a_log.astype(F32), dt_bias.astype(F32), *([p3] * 9), conv_w, conv_w, conv_w, ba_col, ba_row)


def _gdn_scan_body(*refs, hb):
    ins = refs[:12]
    of_ref, ob_ref, s_ref = refs[12], refs[13], refs[14]
    o_refs = (of_ref, ob_ref)

    @pl.when(pl.program_id(2) == 0)
    def _():
        s_ref[...] = jnp.zeros_like(s_ref)

    chains = [(d, hh) for d in range(2) for hh in range(hb)]
    cols = lambda hh: slice(hh * GDN_D, (hh + 1) * GDN_D)
    st = {c: s_ref[c[0], c[1]] for c in chains}
    stb = {c: st[c].astype(BF16) for c in chains}
    ws = {(d, hh): _dot(ins[6 * d + 1][0, 0, :, cols(hh)], stb[(d, hh)]) for d, hh in chains}
    qs = {(d, hh): _dot(ins[6 * d + 3][0, 0, :, cols(hh)], stb[(d, hh)]) for d, hh in chains}
    v_new = {(d, hh): (ins[6 * d][0, 0, :, cols(hh)] - ws[(d, hh)]).astype(BF16) for d, hh in chains}
    for d, hh in chains:
        o_refs[d][0, :, cols(hh)] = qs[(d, hh)] + _dot(ins[6 * d + 2][0, 0, hh], v_new[(d, hh)])
    for d, hh in chains:
        s_ref[d, hh] = (st[(d, hh)] * ins[6 * d + 5][0, 0, hh, 0]
                        + _dot_tn(ins[6 * d + 4][0, 0, :, cols(hh)], v_new[(d, hh)]))


def gdn_scan(u, w, qk, qg, kg, eg):
    _, b, s, _ = u.shape
    c_ = GDN_CHUNK
    n = s // c_
    hb = GDN_HEADS
    def dir_specs(d):
        ci = (lambda i: i) if d == 0 else (lambda i: n - 1 - i)
        row = pl.BlockSpec((1, 1, c_, hb * GDN_D), lambda b_, g, i: (d, b_, ci(i), g))
        return [row, row,
                pl.BlockSpec((1, 1, hb, c_, c_), lambda b_, g, i: (d, b_, g, ci(i), 0)),
                row, row,
                pl.BlockSpec((1, 1, hb, 1, 1, LANES), lambda b_, g, i: (d, b_, g, ci(i), 0, 0))]

    in_specs = dir_specs(0) + dir_specs(1)
    args = [u, w, qk, qg, kg, eg] * 2
    out_f = pl.BlockSpec((1, c_, hb * GDN_D), lambda b_, g, i: (b_, i, g))
    out_b = pl.BlockSpec((1, c_, hb * GDN_D), lambda b_, g, i: (b_, n - 1 - i, g))
    return pl.pallas_call(
        functools.partial(_gdn_scan_body, hb=hb),
        grid=(b, GDN_HEADS // hb, n),
        in_specs=in_specs,
        out_specs=[out_f, out_b],
        out_shape=[jax.ShapeDtypeStruct((b, s, GDN_W), F32)] * 2,
        scratch_shapes=[pltpu.VMEM((2, hb, GDN_D, GDN_D), F32)],
        compiler_params=_cparams("parallel", "parallel", "arbitrary"),
        name="gdn_scan",
    )(*args)


def _gdn_out_body(of_ref, ob_ref, z_ref, nw_ref, o_ref):
    o = of_ref[...] + ob_ref[...]
    z = z_ref[...]
    o = o * lax.rsqrt(jnp.mean(o * o, axis=-1, keepdims=True) + EPS) * nw_ref[...]
    o_ref[...] = (o * (z * jax.nn.sigmoid(z))).astype(o_ref.dtype)


def gdn_out(o_f, o_b, p2, norm_w):
    n, _ = o_f.shape
    ts = min(1024, n)
    blk = lambda off: pl.BlockSpec((ts, LANES), lambda i, h: (i, off + h))
    return pl.pallas_call(
        _gdn_out_body,
        grid=(n // ts, GDN_HEADS),
        in_specs=[blk(0), blk(0), blk(P_Z // LANES), pl.BlockSpec((1, LANES), lambda i, h: (0, 0))],
        out_specs=blk(0),
        out_shape=jax.ShapeDtypeStruct((n, GDN_W), BF16),
        compiler_params=_cparams("parallel", "parallel"),
        name="gdn_out",
    )(o_f, o_b, p2, norm_w.reshape(1, LANES).astype(F32))


def gdn_mixer(p3, conv_w, a_log, dt_bias, norm_w):
    b, s, _ = p3.shape
    ba = p3[..., P_BA:P_BA + 4 * GDN_HEADS].reshape(b, s, 4, GDN_HEADS)
    ba_col = jnp.transpose(ba, (0, 3, 1, 2))
    ba_row = jnp.transpose(ba, (0, 3, 2, 1))
    u, w, qk, qg, kg, eg = gdn_wy(p3, conv_w, ba_col, ba_row, a_log, dt_bias)
    o_f, o_b = gdn_scan(u, w, qk, qg, kg, eg)
    return gdn_out(o_f.reshape(b * s, GDN_W), o_b.reshape(b * s, GDN_W), p3.reshape(b * s, P_COLS), norm_w)


DFT_ROWS = 2048
TABLE_SPLIT = 64


def _angle_tables(rows, cols, period):
    k = lax.broadcasted_iota(jnp.int32, (rows, cols), 0)
    s = lax.broadcasted_iota(jnp.int32, (rows, cols), 1)
    if rows % TABLE_SPLIT or rows <= TABLE_SPLIT:
        ang = ((k * s) % period).astype(F32) * (2.0 * math.pi / period)
        return k, s, jnp.cos(ang), jnp.sin(ang)
    nhi = rows // TABLE_SPLIT
    _, _, ca, sa = _angle_tables_scaled(nhi, cols, period, TABLE_SPLIT)
    _, _, cb, sb = _angle_tables_scaled(TABLE_SPLIT, cols, period, 1)
    c = (ca[:, None, :] * cb[None, :, :] - sa[:, None, :] * sb[None, :, :]).reshape(rows, cols)
    sn = (sa[:, None, :] * cb[None, :, :] + ca[:, None, :] * sb[None, :, :]).reshape(rows, cols)
    return k, s, c, sn


def _angle_tables_scaled(rows, cols, period, mult):
    k = lax.broadcasted_iota(jnp.int32, (rows, cols), 0) * mult
    s = lax.broadcasted_iota(jnp.int32, (rows, cols), 1)
    ang = ((k * s) % period).astype(F32) * (2.0 * math.pi / period)
    return k, s, jnp.cos(ang), jnp.sin(ang)


def hyena_tables(s):
    k, t, c, sn = _angle_tables(s, s, 2 * s)
    alt = lambda idx: (1 - 2 * (idx % 2)).astype(F32)
    fwd = jnp.stack([c, jnp.where(k == 0, alt(t), sn)]).astype(BF16)
    inv = jnp.stack([c, jnp.where(t == 0, alt(k), sn)]).astype(BF16)
    return fwd, inv


def fnet_tables(s):
    _, _, c, sn = _angle_tables(s, s, s)
    return jnp.stack([c, sn]).astype(BF16)


def _seq_dft_body(*refs, nk, epilogue):
    t_ref, x_ref = refs[0], refs[1]
    acc_ref = refs[-1]
    o_ref = refs[-2]
    extra = refs[2:-2]
    kk = pl.program_id(2)

    @pl.when(kk == 0)
    def _():
        acc_ref[...] = jnp.zeros_like(acc_ref)

    xb = x_ref[0].astype(BF16)
    acc_ref[0] += _dot(t_ref[0], xb)
    acc_ref[1] += _dot(t_ref[1], xb)

    @pl.when(kk == nk - 1)
    def _():
        epilogue(acc_ref, extra, o_ref)


def _seq_dft(table, x, wd, col_blk, extra_args, extra_specs, epilogue, out_shape, out_spec, tf, tk, name):
    b, s, _ = x.shape
    tf = min(tf, s)
    tk = min(tk, s)
    nk = s // tk
    return pl.pallas_call(
        functools.partial(_seq_dft_body, nk=nk, epilogue=epilogue),
        grid=(b, s // tf, nk),
        in_specs=[pl.BlockSpec((2, tf, tk), lambda b_, f, kk: (0, f, kk)),
                  pl.BlockSpec((1, tk, wd), lambda b_, f, kk: (b_, kk, col_blk))] + extra_specs,
        out_specs=out_spec,
        out_shape=out_shape,
        scratch_shapes=[pltpu.VMEM((2, tf, wd), F32)],
        compiler_params=_cparams("parallel", "parallel", "arbitrary"),
        name=name,
    )(table, x, *extra_args)


def _hy_prep_body(*refs, ns):
    i = pl.program_id(1)
    w_ref, b_ref = refs[9], refs[10]
    vg_ref, x1_ref = refs[11], refs[12]
    outs = []
    for part in range(3):
        x_ref, xp_ref, xn_ref = refs[3 * part:3 * part + 3]
        outs.append(_conv3(x_ref[0], xp_ref[0], xn_ref[0], w_ref[part], i, ns) + b_ref[part])
    x1, x2, v = outs
    vg_ref[0] = v * x2
    x1_ref[0] = x1


def hy_prep(p3, conv_w, conv_b):
    b, s, _ = p3.shape
    ts = min(512, s)
    ns = s // ts
    ng = HY_W // LANES
    base = P_HY // LANES
    in_specs = []
    for part in range(3):
        in_specs += _halo_specs(ts, s, functools.partial(lambda part_, j: base + part_ * ng + j, part))
    in_specs += [pl.BlockSpec((3, 3, LANES), lambda b_, i, j: (0, 0, j)),
                 pl.BlockSpec((3, 1, LANES), lambda b_, i, j: (0, 0, j))]
    out = pl.BlockSpec((1, ts, LANES), lambda b_, i, j: (b_, i, j))
    cw = jnp.transpose(conv_w.astype(F32).reshape(3, 3, HY_W), (1, 0, 2))
    cb = conv_b.astype(F32).reshape(3, 1, HY_W)
    return pl.pallas_call(
        functools.partial(_hy_prep_body, ns=ns),
        grid=(b, ns, ng),
        in_specs=in_specs,
        out_specs=[out, out],
        out_shape=[jax.ShapeDtypeStruct((b, s, HY_W), F32)] * 2,
        compiler_params=_cparams("parallel", "parallel", "parallel"),
        name="hy_prep",
    )(*([p3] * 9), cw, cb)


def _hy_filter_body(z_ref, w1_ref, b1_ref, w2_ref, b2_ref, w3_ref, b3_ref, w4_ref, b4_ref, fr_ref, dl_ref,
                    buf_ref, sum_ref, *, s, tr):
    i = pl.program_id(0)
    z = z_ref[...]
    fr = fr_ref[...]
    hid = jnp.sin(fr * (_dot(z, w1_ref[...], HIGHEST) + b1_ref[...]))
    hid = jnp.sin(fr * (_dot(hid, w2_ref[...], HIGHEST) + b2_ref[...]))
    hid = jnp.sin(fr * (_dot(hid, w3_ref[...], HIGHEST) + b3_ref[...]))
    hh = _dot(hid, w4_ref[...], HIGHEST) + b4_ref[...]
    window = jnp.exp(-z[:, 0:1] * dl_ref[...])
    row = i * tr + lax.broadcasted_iota(jnp.int32, (tr, HY_W), 0)
    val = jnp.where(row < s, hh[:, :HY_W], hh[:, HY_W:]) * window
    val = jnp.where(row == s, 0.0, val)
    buf_ref[...] = val

    @pl.when(i == 0)
    def _():
        sum_ref[...] = jnp.zeros_like(sum_ref)

    sum_ref[...] += jnp.sum(jnp.abs(val), axis=0, keepdims=True)


def hy_filter(s, w1, b1, w2, b2, w3, b3, w4, b4, freq):
    n2 = 2 * s
    slot = jnp.arange(n2)
    pos = jnp.where(slot <= s, jnp.minimum(slot, s - 1), n2 - slot)
    t_all = jnp.linspace(0.0, 1.0, s, dtype=F32)
    ang_all = (2.0 * math.pi / s) * jnp.arange(s, dtype=F32)
    t = t_all[pos][:, None]
    ang = ang_all[pos][:, None]
    f = jnp.linspace(1e-4, HY_BANDS - 1, HY_BANDS, dtype=F32)[None, :]
    z = jnp.concatenate([t, jnp.cos(f * ang), -jnp.sin(f * ang)], axis=-1)
    z = jnp.pad(z, ((0, 0), (0, LANES - HY_EMB)))
    w1p = jnp.pad(w1.astype(F32), ((0, LANES - HY_EMB), (0, 0)))
    max_decay = math.log(1e-2) / 0.3
    min_decay = math.log(1e-2) / 1.5
    deltas = jnp.abs(jnp.linspace(min_decay, max_decay, HY_W, dtype=F32))[None, :]
    tr = min(512, n2)
    full = lambda a: pl.BlockSpec(a.shape, lambda i: (0,) * a.ndim)
    r2 = lambda a: a.astype(F32).reshape(1, -1)
    args = [z, w1p, r2(b1), w2.astype(F32), r2(b2), w3.astype(F32), r2(b3), w4.astype(F32), r2(b4), r2(freq), deltas]
    return pl.pallas_call(
        functools.partial(_hy_filter_body, s=s, tr=tr),
        grid=(n2 // tr,),
        in_specs=[pl.BlockSpec((tr, LANES), lambda i: (i, 0))] + [full(a) for a in args[1:]],
        out_specs=[pl.BlockSpec((tr, HY_W), lambda i: (i, 0)), pl.BlockSpec((1, HY_W), lambda i: (0, 0))],
        out_shape=[jax.ShapeDtypeStruct((n2, HY_W), F32), jax.ShapeDtypeStruct((1, HY_W), F32)],
        compiler_params=_cparams("arbitrary"),
        name="hy_filter",
    )(*args)


def _spec_raw_epilogue(acc_ref, extra, o_ref):
    o_ref[0] = acc_ref[...]


def _hy_spec_combine_body(a_ref, sum_ref, o_ref, *, tf):
    f = pl.program_id(0)
    k = f * tf + lax.broadcasted_iota(jnp.int32, (tf, HY_W), 0)
    sign = (1 - 2 * (k % 2)).astype(F32)
    inv = 1.0 / sum_ref[...]
    first = a_ref[0]
    second = a_ref[1]
    o_ref[0] = (first[0] + sign * second[0]) * inv
    sign1 = jnp.where(k == 0, 1.0, sign)
    o_ref[1] = (first[1] + sign1 * second[1]) * inv


def hy_filter_spectrum(buf, colsum, fwd_table, s):
    halves = buf.reshape(2, s, HY_W)
    tf = min(DFT_ROWS, s)
    raw = _seq_dft(fwd_table, halves, HY_W, 0, [], [], _spec_raw_epilogue,
                   jax.ShapeDtypeStruct((2, 2, s, HY_W), F32),
                   pl.BlockSpec((1, 2, tf, HY_W), lambda b_, f, kk: (b_, 0, f, 0)), tf, 512, "hy_filter_dft")
    tf = min(512, s)
    return pl.pallas_call(
        functools.partial(_hy_spec_combine_body, tf=tf),
        grid=(s // tf,),
        in_specs=[pl.BlockSpec((2, 2, tf, HY_W), lambda f: (0, 0, f, 0)), pl.BlockSpec((1, HY_W), lambda f: (0, 0))],
        out_specs=pl.BlockSpec((2, tf, HY_W), lambda f: (0, f, 0)),
        out_shape=jax.ShapeDtypeStruct((2, s, HY_W), F32),
        compiler_params=_cparams("parallel"),
        name="hy_filter_combine",
    )(raw, colsum)


def _hy_fwd_epilogue(acc_ref, extra, o_ref, *, s, tf):
    h_ref = extra[0]
    f = pl.program_id(1)
    xc, xs = acc_ref[0], acc_ref[1]
    hc, hs = h_ref[0], h_ref[1]
    row0 = (f * tf + lax.broadcasted_iota(jnp.int32, xc.shape, 0)) == 0
    scale = jnp.where(row0, 1.0 / (2 * s), 2.0 / (2 * s))
    yc = jnp.where(row0, xc * hc, xc * hc - xs * hs) * scale
    ys = jnp.where(row0, xs * hs, xc * hs + xs * hc) * scale
    o_ref[0, 0] = yc.astype(o_ref.dtype)
    o_ref[0, 1] = ys.astype(o_ref.dtype)


def _hy_inv_body(t_ref, y_ref, vg_ref, x1_ref, skip_ref, o_ref, acc_ref, *, nk):
    kk = pl.program_id(2)

    @pl.when(kk == 0)
    def _():
        acc_ref[...] = jnp.zeros_like(acc_ref)

    acc_ref[...] += _dot(t_ref[0], y_ref[0, 0]) + _dot(t_ref[1], y_ref[0, 1])

    @pl.when(kk == nk - 1)
    def _():
        o_ref[0] = ((acc_ref[...] + vg_ref[0] * skip_ref[...]) * x1_ref[0]).astype(o_ref.dtype)


def hyena_mixer(p3, conv_w, conv_b, spectrum, skip, fwd_table, inv_table):
    b, s, _ = p3.shape
    vg, x1 = hy_prep(p3, conv_w, conv_b)
    tf = min(DFT_ROWS, s)
    tk = min(512, s)
    yp = _seq_dft(fwd_table, vg, HY_W, 0, [spectrum], [pl.BlockSpec((2, tf, HY_W), lambda b_, f, kk: (0, f, 0))],
                  functools.partial(_hy_fwd_epilogue, s=s, tf=tf),
                  jax.ShapeDtypeStruct((b, 2, s, HY_W), BF16),
                  pl.BlockSpec((1, 2, tf, HY_W), lambda b_, f, kk: (b_, 0, f, 0)), tf, tk, "hy_fwd_dft")
    nk = s // tk
    tile = pl.BlockSpec((1, tf, HY_W), lambda b_, f, kk: (b_, f, 0))
    return pl.pallas_call(
        functools.partial(_hy_inv_body, nk=nk),
        grid=(b, s // tf, nk),
        in_specs=[pl.BlockSpec((2, tf, tk), lambda b_, f, kk: (0, f, kk)),
                  pl.BlockSpec((1, 2, tk, HY_W), lambda b_, f, kk: (b_, 0, kk, 0)),
                  tile, tile, pl.BlockSpec((1, HY_W), lambda b_, f, kk: (0, 0))],
        out_specs=tile,
        out_shape=jax.ShapeDtypeStruct((b, s, HY_W), BF16),
        scratch_shapes=[pltpu.VMEM((tf, HY_W), F32)],
        compiler_params=_cparams("parallel", "parallel", "arbitrary"),
        name="hy_inv_dft",
    )(inv_table, yp, vg, x1, skip.astype(F32).reshape(1, HY_W))


def _fnet_epilogue(acc_ref, extra, o_ref):
    cw_ref = extra[0]
    for g in range(FN_GROUPS):
        cs = slice(g * FN_GC, (g + 1) * FN_GC)
        y = _dot(acc_ref[0, :, cs].astype(BF16), cw_ref[0, g]) + _dot(acc_ref[1, :, cs].astype(BF16), cw_ref[1, g])
        o_ref[0, :, cs] = y.astype(o_ref.dtype)


def _fnet_fold_body(t_ref, w_ref, o_ref, *, scale):
    for g in range(FN_GROUPS):
        o_ref[0, g] = (_dot(t_ref[0], w_ref[g], HIGHEST) * scale).astype(o_ref.dtype)
        o_ref[1, g] = (_dot(t_ref[1], w_ref[g], HIGHEST) * (-scale)).astype(o_ref.dtype)


def fnet_mixer(p3, w, table):
    b, s, _ = p3.shape
    _, _, cc, sc = _angle_tables(FN_GC, FN_GC, FN_GC)
    folded = pl.pallas_call(
        functools.partial(_fnet_fold_body, scale=1.0 / math.sqrt(s * FN_GC)),
        out_shape=jax.ShapeDtypeStruct((2, FN_GROUPS, FN_GC, FN_GC), BF16),
        name="fnet_fold",
    )(jnp.stack([cc, sc]), w.astype(F32))
    tf = min(DFT_ROWS, s)
    return _seq_dft(table, p3, FN_W, P_FN // FN_W, [folded],
                    [pl.BlockSpec((2, FN_GROUPS, FN_GC, FN_GC), lambda b_, f, kk: (0, 0, 0, 0))],
                    _fnet_epilogue, jax.ShapeDtypeStruct((b, s, FN_W), BF16),
                    pl.BlockSpec((1, tf, FN_W), lambda b_, f, kk: (b_, f, 0)), tf, 512, "fnet_dft")


_CAND_GROUPS = [16, 8, 5, 4, 3, 2, 2, 2]
_CAND_ROWS = sum(-(-g // 8) * 8 for g in _CAND_GROUPS) + 8


def _extract_top(cur_list, val_refs, idx_refs, k):
    rows = [lax.broadcasted_iota(jnp.int32, c.shape, 0) for c in cur_list]

    def step(r, curs):
        out = []
        for t, cur in enumerate(curs):
            m = jnp.max(cur, axis=0, keepdims=True)
            first = jnp.min(jnp.where(cur == m, rows[t], cur.shape[0]), axis=0, keepdims=True)
            if val_refs[t] is not None:
                val_refs[t][pl.ds(r, 1), :] = m
            if idx_refs[t] is not None:
                idx_refs[t][pl.ds(r, 1), :] = first
            out.append(jnp.where(rows[t] == first, -jnp.inf, cur))
        return tuple(out)

    return lax.fori_loop(0, k, step, tuple(cur_list))


def _bf16_bits(x):
    bits = pltpu.bitcast(x, jnp.uint32)
    return (bits + jnp.uint32(0x7FFF) + ((bits >> 16) & jnp.uint32(1))) >> 16


def _peer_score_body(q_ref, k1_ref, k2_ref, cnt_ref, e1_ref, rank2_ref, e2_ref, v_ref, idx_ref, sc_ref, c_ref,
                     pair_ref):
    kk = PEER_TOPK
    tt = q_ref.shape[0]
    nlc = tt // LANES
    q = q_ref[...]
    s1 = _dot_nt(k1_ref[...], q[:, :PEER_HALF].astype(BF16))
    s2 = _dot_nt(k2_ref[...], q[:, PEER_HALF:].astype(BF16))
    for lc in range(nlc):
        ls = slice(lc * LANES, (lc + 1) * LANES)
        _extract_top([s1[:, ls], s2[:, ls]], [v_ref.at[0, lc], v_ref.at[1, lc]],
                     [idx_ref.at[0, lc], idx_ref.at[1, lc]], kk)
    rows8 = lax.broadcasted_iota(jnp.int32, (8, LANES), 0)
    cands = []
    for lc in range(nlc):
        v1 = v_ref[0, lc]
        v2 = v_ref[1, lc]
        pieces = []
        for i, g in enumerate(_CAND_GROUPS):
            gp = -(-g // 8) * 8
            piece = v1[i:i + 1, :] + v2[0:gp, :]
            pieces.append(piece if g == gp else jnp.where(rows8 < g, piece, -jnp.inf))
        pieces.append(v1[8:16, :] + v2[0:1, :])
        cands.append(jnp.concatenate(pieces, axis=0))
    finals = _extract_top(cands, [sc_ref.at[lc] for lc in range(nlc)], [None] * nlc, kk)
    rows = lax.broadcasted_iota(jnp.int32, (PEER_NKEYS, LANES), 0)
    for lc in range(nlc):
        ls = slice(lc * LANES, (lc + 1) * LANES)
        taken = jnp.where((finals[lc] == -jnp.inf) & (cands[lc] > -jnp.inf), 1.0, 0.0)
        off = 0
        for i, g in enumerate(_CAND_GROUPS):
            gp = -(-g // 8) * 8
            c_ref[i:i + 1, :] = jnp.sum(taken[off:off + gp], axis=0, keepdims=True)
            off += gp
        c_ref[8:16, :] = taken[off:off + 8]
        sc = sc_ref[lc]
        z = jnp.sum(jnp.exp(sc - sc[0:1, :]), axis=0, keepdims=True)
        cnt = jnp.zeros((PEER_NKEYS, LANES), F32)
        rank2 = jnp.full((PEER_NKEYS, LANES), float(kk), F32)
        for r in range(kk):
            cnt = jnp.where(rows == idx_ref[0, lc, r:r + 1, :], c_ref[r:r + 1, :], cnt)
            rank2 = jnp.where(rows == idx_ref[1, lc, r:r + 1, :], float(r), rank2)
        cnt_bits = _bf16_bits(cnt)
        e1_bits = _bf16_bits(jnp.exp(s1[:, ls] - v_ref[0, lc, 0:1, :]) / z)
        cnt_ref[0, :, ls] = cnt_bits | (cnt_bits << 16)
        e1_ref[0, :, ls] = e1_bits | (e1_bits << 16)
        pair_ref[0] = _bf16_bits(rank2)
        pair_ref[1] = _bf16_bits(jnp.exp(s2[:, ls] - v_ref[1, lc, 0:1, :]))
        even = pl.ds(0, PEER_NKEYS // 2, stride=2)
        odd = pl.ds(1, PEER_NKEYS // 2, stride=2)
        rank2_ref[0, :, ls] = pair_ref[0, even, :] | (pair_ref[0, odd, :] << 16)
        e2_ref[0, :, ls] = pair_ref[1, even, :] | (pair_ref[1, odd, :] << 16)


def peer_scores(q, k1, k2):
    n, _ = q.shape
    tt = min(512, n)
    nlc = tt // LANES
    kk = PEER_TOPK
    rows = jax.ShapeDtypeStruct((PEER_HEADS, PEER_NKEYS, n), jnp.uint32)
    tiles = jax.ShapeDtypeStruct((PEER_HEADS, PEER_NKEYS // 2, n), jnp.uint32)
    blk = pl.BlockSpec((1, PEER_NKEYS, tt), lambda i, h: (h, 0, i))
    blk2 = pl.BlockSpec((1, PEER_NKEYS // 2, tt), lambda i, h: (h, 0, i))
    keys = pl.BlockSpec((PEER_NKEYS, PEER_HALF), lambda i, h: (0, 0))
    return pl.pallas_call(
        _peer_score_body,
        grid=(n // tt, PEER_HEADS),
        in_specs=[pl.BlockSpec((tt, 2 * PEER_HALF), lambda i, h: (i, h)), keys, keys],
        out_specs=[blk, blk, blk2, blk2],
        out_shape=[rows, rows, tiles, tiles],
        scratch_shapes=[pltpu.VMEM((2, nlc, kk, LANES), F32), pltpu.VMEM((2, nlc, kk, LANES), jnp.int32),
                        pltpu.VMEM((nlc, kk, LANES), F32), pltpu.VMEM((kk, LANES), F32),
                        pltpu.VMEM((2, PEER_NKEYS, LANES), jnp.uint32)],
        compiler_params=_cparams("parallel", "parallel"),
        name="peer_scores",
    )(q, k1.astype(BF16), k2.astype(BF16))


BF16_ROWS = 16


def _peer_dense_body(h_ref, u_ref, vt_ref, cnt_ref, e1_ref, rank2_ref, e2_ref, o_ref, acc_ref, gl_ref, ga_ref,
                     *, ne, r):
    e = pl.program_id(1)
    tt = h_ref.shape[0]
    nblk = PEER_NKEYS // BF16_ROWS

    @pl.when(e == 0)
    def _():
        acc_ref[...] = jnp.zeros_like(acc_ref)

    act = _dot_nt(u_ref[...], h_ref[...])
    gl_ref[...] = (0.5 * act * (1.0 + lax.erf(act * (1.0 / math.sqrt(2.0))))).astype(BF16)
    for rr in range(r):
        e1 = e * r + rr
        cnt_rows = [cnt_ref[hd, pl.ds(e1, 1), :] for hd in range(PEER_HEADS)]
        e1_rows = [e1_ref[hd, pl.ds(e1, 1), :] for hd in range(PEER_HEADS)]
        for lc in range(tt // LANES):
            ls = slice(lc * LANES, (lc + 1) * LANES)
            gate = [None] * nblk
            words = BF16_ROWS // 2
            for hd in range(PEER_HEADS):
                cnt16 = pltpu.bitcast(jnp.broadcast_to(cnt_rows[hd][:, ls], (words, LANES)), BF16)
                w16 = pltpu.bitcast(jnp.broadcast_to(e1_rows[hd][:, ls], (words, LANES)), BF16)
                for kb in range(nblk):
                    ks = slice(kb * words, (kb + 1) * words)
                    rank2 = pltpu.bitcast(rank2_ref[hd, ks, ls], BF16)
                    term = jnp.where(rank2 < cnt16, pltpu.bitcast(e2_ref[hd, ks, ls], BF16), 0.0) * w16
                    gate[kb] = term if gate[kb] is None else gate[kb] + term
            for kb in range(nblk):
                es = slice(rr * PEER_NKEYS + kb * BF16_ROWS, rr * PEER_NKEYS + (kb + 1) * BF16_ROWS)
                ga_ref[es, ls] = gl_ref[es, ls] * gate[kb]
    acc_ref[...] += _dot(vt_ref[...], ga_ref[...])

    @pl.when(e == ne - 1)
    def _():
        o_ref[...] = acc_ref[...].T


def peer_dense(h, u_bf, vt_bf, cnt, e1, rank2, e2):
    n, d = h.shape
    ne_total = u_bf.shape[0]
    tt = min(512, n)
    r = 8
    te = r * PEER_NKEYS
    ne = ne_total // te
    tok = pl.BlockSpec((PEER_HEADS, PEER_NKEYS, tt), lambda i, e: (0, 0, i))
    tok2 = pl.BlockSpec((PEER_HEADS, PEER_NKEYS // 2, tt), lambda i, e: (0, 0, i))
    return pl.pallas_call(
        functools.partial(_peer_dense_body, ne=ne, r=r),
        grid=(n // tt, ne),
        in_specs=[pl.BlockSpec((tt, d), lambda i, e: (i, 0)),
                  pl.BlockSpec((te, d), lambda i, e: (e, 0)),
                  pl.BlockSpec((d, te), lambda i, e: (0, e)),
                  tok, tok, tok2, tok2],
        out_specs=pl.BlockSpec((tt, d), lambda i, e: (i, 0)),
        out_shape=jax.ShapeDtypeStruct((n, d), F32),
        scratch_shapes=[pltpu.VMEM((d, tt), F32), pltpu.VMEM((te, tt), BF16), pltpu.VMEM((te, tt), BF16)],
        compiler_params=_cparams("parallel", "arbitrary"),
        name="peer_dense",
    )(h, u_bf, vt_bf, cnt, e1, rank2, e2)


def peer_ffn(x2d, norm_w, wq, k1, k2, u_tab, v_tab):
    h = rmsnorm(x2d, norm_w, BF16)
    q = linear([(h, wq.astype(BF16))], None, F32)
    cnt, e1, rank2, e2 = peer_scores(q, k1, k2)
    return peer_dense(h, u_tab.astype(BF16), v_tab.astype(BF16).T, cnt, e1, rank2, e2)


def _reorder_w_in(w):
    off_z, off_ba = 3 * GDN_W, 4 * GDN_W
    off_hy = off_ba + 4 * GDN_HEADS
    off_fn = off_hy + 3 * HY_W
    pad = jnp.zeros((w.shape[0], LANES - 4 * GDN_HEADS), w.dtype)
    return jnp.concatenate([w[:, :off_z], w[:, off_z:off_ba], w[:, off_hy:off_fn], w[:, off_fn:],
                            w[:, off_ba:off_hy], pad], axis=1)


def kernel(x, norm1_w, w_in, gdn_conv_w, gdn_a_log, gdn_dt_bias, gdn_norm_w, hy_conv_w, hy_conv_b, hy_w1, hy_b1, hy_w2, hy_b2, hy_w3, hy_b3, hy_w4, hy_b4, hy_freq, hy_skip, fnet_w, w_out, norm2_w, peer_wq, peer_k1, peer_k2, peer_u, peer_v, final_norm_w):
    b, s, d = x.shape
    n = b * s
    depth = w_in.shape[0]
    hy_fwd, hy_inv = hyena_tables(s)
    fn_tab = fnet_tables(s)
    x2d = x.reshape(n, d)
    delta = None
    for l in range(depth):
        if delta is None:
            h = rmsnorm(x2d, norm1_w[l], BF16)
        else:
            x2d, h = add_rmsnorm(x2d, delta, norm1_w[l], BF16, True)
        p = linear([(h, _reorder_w_in(w_in[l]).astype(BF16))], None, F32, tm=2048, tn=896)
        p3 = p.reshape(b, s, P_COLS)
        y_a = gdn_mixer(p3, gdn_conv_w[l], gdn_a_log[l], gdn_dt_bias[l], gdn_norm_w[l])
        buf, colsum = hy_filter(s, hy_w1[l], hy_b1[l], hy_w2[l], hy_b2[l], hy_w3[l], hy_b3[l], hy_w4[l], hy_b4[l],
                                hy_freq[l])
        spectrum = hy_filter_spectrum(buf, colsum, hy_fwd, s)
        y_b = hyena_mixer(p3, hy_conv_w[l], hy_conv_b[l], spectrum, hy_skip[l], hy_fwd, hy_inv)
        y_c = fnet_mixer(p3, fnet_w[l], fn_tab)
        wo = w_out[l].astype(BF16)
        x2d = linear([(y_a, wo[:GDN_W]), (y_b.reshape(n, HY_W), wo[GDN_W:GDN_W + HY_W]),
                      (y_c.reshape(n, FN_W), wo[GDN_W + HY_W:])], x2d, F32)
        delta = peer_ffn(x2d, norm2_w[l], peer_wq[l], peer_k1[l], peer_k2[l], peer_u[l], peer_v[l])
    return add_rmsnorm(x2d, delta, final_norm_w, F32, False).reshape(b, s, d)
```

```python
import functools
import math

import jax
import jax.numpy as jnp
from jax import lax
from jax.experimental import pallas as pl
from jax.experimental.pallas import tpu as pltpu

F32 = jnp.float32
BF16 = jnp.bfloat16
EPS = 1e-6
LANES = 128
VMEM_LIMIT_BYTES = 56 * 1024 * 1024
HIGHEST = lax.Precision.HIGHEST

GDN_HEADS = 8
GDN_D = 128
GDN_W = GDN_HEADS * GDN_D
GDN_CHUNK = 64
HY_W = 512
HY_EMB = 33
HY_BANDS = 16
HY_FFN = 64
FN_W = 512
FN_GROUPS = 4
FN_GC = 128
PEER_HEADS = 8
PEER_NKEYS = 128
PEER_HALF = 128
PEER_TOPK = 16
P_QKV = 0
P_Z = 3 * GDN_W
P_HY = P_Z + GDN_W
P_FN = P_HY + 3 * HY_W
P_BA = P_FN + FN_W
P_COLS = P_BA + LANES


def _cparams(*sem):
    return pltpu.CompilerParams(dimension_semantics=sem, vmem_limit_bytes=VMEM_LIMIT_BYTES)


def _dot(a, b, precision=None):
    return jnp.dot(a, b, preferred_element_type=F32, precision=precision)


def _dot_nt(a, b, precision=None):
    return lax.dot_general(a, b, (((1,), (1,)), ((), ())), preferred_element_type=F32, precision=precision)


def _dot_tn(a, b, precision=None):
    return lax.dot_general(a, b, (((0,), (0,)), ((), ())), preferred_element_type=F32, precision=precision)


def _rms_body(x_ref, w_ref, o_ref):
    x = x_ref[...]
    ms = jnp.mean(x * x, axis=-1, keepdims=True)
    o_ref[...] = (x * lax.rsqrt(ms + EPS) * w_ref[...]).astype(o_ref.dtype)


def rmsnorm(x2d, w, out_dtype):
    n, d = x2d.shape
    tm = min(256, n)
    return pl.pallas_call(
        _rms_body,
        grid=(n // tm,),
        in_specs=[pl.BlockSpec((tm, d), lambda i: (i, 0)), pl.BlockSpec((1, d), lambda i: (0, 0))],
        out_specs=pl.BlockSpec((tm, d), lambda i: (i, 0)),
        out_shape=jax.ShapeDtypeStruct((n, d), out_dtype),
        compiler_params=_cparams("parallel"),
        name="rmsnorm",
    )(x2d, w.reshape(1, d).astype(F32))


def _add_rms_body(x_ref, y_ref, w_ref, *o_refs):
    x = x_ref[...] + y_ref[...]
    ms = jnp.mean(x * x, axis=-1, keepdims=True)
    o_refs[-1][...] = (x * lax.rsqrt(ms + EPS) * w_ref[...]).astype(o_refs[-1].dtype)
    if len(o_refs) == 2:
        o_refs[0][...] = x


def add_rmsnorm(x2d, y2d, w, out_dtype, emit_sum):
    n, d = x2d.shape
    tm = min(256, n)
    row = pl.BlockSpec((tm, d), lambda i: (i, 0))
    normed = jax.ShapeDtypeStruct((n, d), out_dtype)
    return pl.pallas_call(
        _add_rms_body,
        grid=(n // tm,),
        in_specs=[row, row, pl.BlockSpec((1, d), lambda i: (0, 0))],
        out_specs=[row, row] if emit_sum else row,
        out_shape=[jax.ShapeDtypeStruct((n, d), F32), normed] if emit_sum else normed,
        compiler_params=_cparams("parallel"),
        name="add_rmsnorm",
    )(x2d, y2d, w.reshape(1, d).astype(F32))


def _linear_body(*refs, npairs, has_res):
    o_ref = refs[-1]
    acc = _dot(refs[0][...], refs[1][...])
    for p in range(1, npairs):
        acc = acc + _dot(refs[2 * p][...], refs[2 * p + 1][...])
    if has_res:
        acc = acc + refs[2 * npairs][...]
    o_ref[...] = acc.astype(o_ref.dtype)


def linear(pairs, res, out_dtype, tm=1024, tn=512):
    m = pairs[0][0].shape[0]
    n = pairs[0][1].shape[1]
    tm = min(tm, m)
    tn = min(tn, n)
    assert m % tm == 0 and n % tn == 0
    in_specs, args = [], []
    for a, b in pairs:
        k = a.shape[1]
        in_specs += [pl.BlockSpec((tm, k), lambda j, i: (i, 0)), pl.BlockSpec((k, tn), lambda j, i: (0, j))]
        args += [a, b]
    if res is not None:
        in_specs.append(pl.BlockSpec((tm, tn), lambda j, i: (i, j)))
        args.append(res)
    return pl.pallas_call(
        functools.partial(_linear_body, npairs=len(pairs), has_res=res is not None),
        grid=(n // tn, m // tm),
        in_specs=in_specs,
        out_specs=pl.BlockSpec((tm, tn), lambda j, i: (i, j)),
        out_shape=jax.ShapeDtypeStruct((m, n), out_dtype),
        compiler_params=_cparams("parallel", "parallel"),
        name="linear",
    )(*args)


def _conv3(x, prev_blk, next_blk, w, i, ns):
    ts = x.shape[0]
    prev_row = jnp.where(i > 0, prev_blk[7:8, :], 0.0)
    next_row = jnp.where(i < ns - 1, next_blk[0:1, :], 0.0)
    rows = lax.broadcasted_iota(jnp.int32, x.shape, 0)
    xm = jnp.where(rows == 0, prev_row, pltpu.roll(x, 1, 0))
    xp = jnp.where(rows == ts - 1, next_row, pltpu.roll(x, ts - 1, 0))
    return xm * w[0:1, :] + x * w[1:2, :] + xp * w[2:3, :]


def _halo_specs(ts, s, col_fn):
    nb8 = s // 8
    r = ts // 8
    cur = pl.BlockSpec((1, ts, LANES), lambda b, i, j: (b, i, col_fn(j)))
    prev = pl.BlockSpec((1, 8, LANES), lambda b, i, j: (b, jnp.maximum(i * r - 1, 0), col_fn(j)))
    nxt = pl.BlockSpec((1, 8, LANES), lambda b, i, j: (b, jnp.minimum((i + 1) * r, nb8 - 1), col_fn(j)))
    return [cur, prev, nxt]


def _gdn_wy_body(alog_ref, dtb_ref, *refs, nchunk, ns):
    xq, xk, xv = refs[0:3], refs[3:6], refs[6:9]
    wq_ref, wk_ref, wv_ref, bac_ref, bar_ref = refs[9:14]
    u_ref, w_ref, qk_ref, qg_ref, kg_ref, eg_ref = refs[14:20]
    q_ref, k_ref, v_ref = refs[20:23]
    c_ = GDN_CHUNK
    h = pl.program_id(1)
    i = pl.program_id(2)

    def prep(x, cw_ref, scale):
        y = _conv3(x[0][0], x[1][0], x[2][0], cw_ref[...], i, ns)
        y = y * jax.nn.sigmoid(y)
        if scale is None:
            return y
        return y * (lax.rsqrt(jnp.sum(y * y, axis=-1, keepdims=True) + EPS) * scale)

    q_ref[0] = prep(xq, wq_ref, GDN_D ** -0.5)
    k_ref[0] = prep(xk, wk_ref, 1.0)
    v_ref[0] = prep(xv, wv_ref, None)
    ii = lax.broadcasted_iota(jnp.int32, (c_, c_), 0)
    jj = lax.broadcasted_iota(jnp.int32, (c_, c_), 1)
    eye = (ii == jj).astype(F32)
    bdot = lambda a, b: _dot(a.astype(BF16), b.astype(BF16))
    gram = []
    for c in range(nchunk):
        sl = slice(c * c_, (c + 1) * c_)
        kb16 = k_ref[0, sl, :].astype(BF16)
        qk16 = jnp.concatenate([q_ref[0, sl, :].astype(BF16), kb16], axis=0)
        gram.append(_dot_nt(qk16, kb16))
    jobs = []
    for d in range(2):
        neg_a = -jnp.exp(jnp.full((1, 1), alog_ref[d, h], F32))
        dtb = dtb_ref[d, h]
        incl = (ii >= jj) if d == 0 else (ii <= jj)
        incl_t = (ii <= jj) if d == 0 else (ii >= jj)
        strict = (ii > jj) if d == 0 else (ii < jj)
        for c in range(nchunk):
            sl = slice(c * c_, (c + 1) * c_)
            beta = jax.nn.sigmoid(bac_ref[0, 0, sl, d:d + 1])
            lg_c = neg_a * jax.nn.softplus(bac_ref[0, 0, sl, 2 + d:3 + d] + dtb)
            lg_r = neg_a * jax.nn.softplus(bar_ref[0, 0, 2 + d:3 + d, sl] + dtb)
            g_c = jnp.sum(jnp.where(incl, lg_r, 0.0), axis=1, keepdims=True)
            g_r = jnp.sum(jnp.where(incl_t, lg_c, 0.0), axis=0, keepdims=True)
            g_tot = jnp.sum(lg_r, axis=1, keepdims=True)
            decay = jnp.where(incl, jnp.exp(jnp.where(incl, g_c - g_r, 0.0)), 0.0)
            qk_ref[d, 0, 0, sl, :] = jnp.where(incl, gram[c][:c_] * decay, 0.0).astype(qk_ref.dtype)
            eg_ref[d, 0, 0, c] = jnp.broadcast_to(jnp.exp(g_tot), (1, LANES))
            p = jnp.where(strict, gram[c][c_:] * decay, 0.0) * (-beta)
            jobs.append(dict(d=d, sl=sl, beta=beta, g_c=g_c, g_tot=g_tot, p=p, x=eye + p))
    for _ in range(int(math.log2(c_)) - 1):
        for jb in jobs:
            jb["p"] = bdot(jb["p"], jb["p"])
        for jb in jobs:
            jb["x"] = jb["x"] + bdot(jb["x"], jb["p"])
    for jb in jobs:
        d, sl, beta, g_c = jb["d"], jb["sl"], jb["beta"], jb["g_c"]
        k = k_ref[0, sl, :]
        kb = k * beta
        rhs = jnp.concatenate([v_ref[0, sl, :] * beta, kb * jnp.exp(g_c)], axis=1)
        sol = bdot(jb["x"], rhs)
        u_ref[d, 0, sl, :] = sol[:, :GDN_D]
        w_ref[d, 0, sl, :] = sol[:, GDN_D:].astype(w_ref.dtype)
        qg_ref[d, 0, sl, :] = (q_ref[0, sl, :] * jnp.exp(g_c)).astype(qg_ref.dtype)
        kg_ref[d, 0, sl, :] = (k * jnp.exp(jb["g_tot"] - g_c)).astype(kg_ref.dtype)


def gdn_wy(p3, conv_w, ba_col, ba_row, a_log, dt_bias):
    b, s, _ = p3.shape
    hh = GDN_HEADS
    ts = min(512, s)
    nchunk = ts // GDN_CHUNK
    n = s // GDN_CHUNK
    nb8 = s // 8
    r8 = ts // 8
    smem = pl.BlockSpec(memory_space=pltpu.SMEM)

    def halo(off):
        return [pl.BlockSpec((1, ts, LANES), lambda b_, h, i: (b_, i, off + h)),
                pl.BlockSpec((1, 8, LANES), lambda b_, h, i: (b_, jnp.maximum(i * r8 - 1, 0), off + h)),
                pl.BlockSpec((1, 8, LANES), lambda b_, h, i: (b_, jnp.minimum((i + 1) * r8, nb8 - 1), off + h))]

    cw = lambda off: pl.BlockSpec((3, LANES), lambda b_, h, i: (0, off + h))
    row_out = pl.BlockSpec((2, 1, ts, LANES), lambda b_, h, i: (0, b_, i, h))
    big = lambda dt: jax.ShapeDtypeStruct((2, b, s, GDN_W), dt)
    conv_w = conv_w.astype(F32)
    return pl.pallas_call(
        functools.partial(_gdn_wy_body, nchunk=nchunk, ns=s // ts),
        grid=(b, hh, s // ts),
        in_specs=[smem, smem] + halo(0) + halo(hh) + halo(2 * hh) + [cw(0), cw(hh), cw(2 * hh),
                  pl.BlockSpec((1, 1, ts, 4), lambda b_, h, i: (b_, h, i, 0)),
                  pl.BlockSpec((1, 1, 4, ts), lambda b_, h, i: (b_, h, 0, i))],
        scratch_shapes=[pltpu.VMEM((1, ts, LANES), F32)] * 3,
        out_specs=[row_out, row_out,
                   pl.BlockSpec((2, 1, 1, ts, GDN_CHUNK), lambda b_, h, i: (0, b_, h, i, 0)),
                   row_out, row_out,
                   pl.BlockSpec((2, 1, 1, nchunk, 1, LANES), lambda b_, h, i: (0, b_, h, i, 0, 0))],
        out_shape=[big(F32), big(BF16), jax.ShapeDtypeStruct((2, b, hh, s, GDN_CHUNK), BF16),
                   big(BF16), big(BF16), jax.ShapeDtypeStruct((2, b, hh, n, 1, LANES), F32)],
        compiler_params=_cparams("parallel", "parallel", "parallel"),
        name="gdn_wy",
    )(a_log.astype(F32), dt_bias.astype(F32), *([p3] * 9), conv_w, conv_w, conv_w, ba_col, ba_row)


def _gdn_scan_body(*refs, hb):
    ins = refs[:12]
    of_ref, ob_ref, s_ref = refs[12], refs[13], refs[14]
    o_refs = (of_ref, ob_ref)

    @pl.when(pl.program_id(2) == 0)
    def _():
        s_ref[...] = jnp.zeros_like(s_ref)

    chains = [(d, hh) for d in range(2) for hh in range(hb)]
    cols = lambda hh: slice(hh * GDN_D, (hh + 1) * GDN_D)
    st = {c: s_ref[c[0], c[1]] for c in chains}
    stb = {c: st[c].astype(BF16) for c in chains}
    ws = {(d, hh): _dot(ins[6 * d + 1][0, 0, :, cols(hh)], stb[(d, hh)]) for d, hh in chains}
    qs = {(d, hh): _dot(ins[6 * d + 3][0, 0, :, cols(hh)], stb[(d, hh)]) for d, hh in chains}
    v_new = {(d, hh): (ins[6 * d][0, 0, :, cols(hh)] - ws[(d, hh)]).astype(BF16) for d, hh in chains}
    for d, hh in chains:
        o_refs[d][0, :, cols(hh)] = qs[(d, hh)] + _dot(ins[6 * d + 2][0, 0, hh], v_new[(d, hh)])
    for d, hh in chains:
        s_ref[d, hh] = (st[(d, hh)] * ins[6 * d + 5][0, 0, hh, 0]
                        + _dot_tn(ins[6 * d + 4][0, 0, :, cols(hh)], v_new[(d, hh)]))


def gdn_scan(u, w, qk, qg, kg, eg):
    _, b, s, _ = u.shape
    c_ = GDN_CHUNK
    n = s // c_
    hb = GDN_HEADS
    def dir_specs(d):
        ci = (lambda i: i) if d == 0 else (lambda i: n - 1 - i)
        row = pl.BlockSpec((1, 1, c_, hb * GDN_D), lambda b_, g, i: (d, b_, ci(i), g))
        return [row, row,
                pl.BlockSpec((1, 1, hb, c_, c_), lambda b_, g, i: (d, b_, g, ci(i), 0)),
                row, row,
                pl.BlockSpec((1, 1, hb, 1, 1, LANES), lambda b_, g, i: (d, b_, g, ci(i), 0, 0))]

    in_specs = dir_specs(0) + dir_specs(1)
    args = [u, w, qk, qg, kg, eg] * 2
    out_f = pl.BlockSpec((1, c_, hb * GDN_D), lambda b_, g, i: (b_, i, g))
    out_b = pl.BlockSpec((1, c_, hb * GDN_D), lambda b_, g, i: (b_, n - 1 - i, g))
    return pl.pallas_call(
        functools.partial(_gdn_scan_body, hb=hb),
        grid=(b, GDN_HEADS // hb, n),
        in_specs=in_specs,
        out_specs=[out_f, out_b],
        out_shape=[jax.ShapeDtypeStruct((b, s, GDN_W), F32)] * 2,
        scratch_shapes=[pltpu.VMEM((2, hb, GDN_D, GDN_D), F32)],
        compiler_params=_cparams("parallel", "parallel", "arbitrary"),
        name="gdn_scan",
    )(*args)


def _gdn_out_body(of_ref, ob_ref, z_ref, nw_ref, o_ref):
    o = of_ref[...] + ob_ref[...]
    z = z_ref[...]
    o = o * lax.rsqrt(jnp.mean(o * o, axis=-1, keepdims=True) + EPS) * nw_ref[...]
    o_ref[...] = (o * (z * jax.nn.sigmoid(z))).astype(o_ref.dtype)


def gdn_out(o_f, o_b, p2, norm_w):
    n, _ = o_f.shape
    ts = min(1024, n)
    blk = lambda off: pl.BlockSpec((ts, LANES), lambda i, h: (i, off + h))
    return pl.pallas_call(
        _gdn_out_body,
        grid=(n // ts, GDN_HEADS),
        in_specs=[blk(0), blk(0), blk(P_Z // LANES), pl.BlockSpec((1, LANES), lambda i, h: (0, 0))],
        out_specs=blk(0),
        out_shape=jax.ShapeDtypeStruct((n, GDN_W), BF16),
        compiler_params=_cparams("parallel", "parallel"),
        name="gdn_out",
    )(o_f, o_b, p2, norm_w.reshape(1, LANES).astype(F32))


def gdn_mixer(p3, conv_w, a_log, dt_bias, norm_w):
    b, s, _ = p3.shape
    ba = p3[..., P_BA:P_BA + 4 * GDN_HEADS].reshape(b, s, 4, GDN_HEADS)
    ba_col = jnp.transpose(ba, (0, 3, 1, 2))
    ba_row = jnp.transpose(ba, (0, 3, 2, 1))
    u, w, qk, qg, kg, eg = gdn_wy(p3, conv_w, ba_col, ba_row, a_log, dt_bias)
    o_f, o_b = gdn_scan(u, w, qk, qg, kg, eg)
    return gdn_out(o_f.reshape(b * s, GDN_W), o_b.reshape(b * s, GDN_W), p3.reshape(b * s, P_COLS), norm_w)


DFT_ROWS = 2048
TABLE_SPLIT = 64


def _angle_tables(rows, cols, period):
    k = lax.broadcasted_iota(jnp.int32, (rows, cols), 0)
    s = lax.broadcasted_iota(jnp.int32, (rows, cols), 1)
    if rows % TABLE_SPLIT or rows <= TABLE_SPLIT:
        ang = ((k * s) % period).astype(F32) * (2.0 * math.pi / period)
        return k, s, jnp.cos(ang), jnp.sin(ang)
    nhi = rows // TABLE_SPLIT
    _, _, ca, sa = _angle_tables_scaled(nhi, cols, period, TABLE_SPLIT)
    _, _, cb, sb = _angle_tables_scaled(TABLE_SPLIT, cols, period, 1)
    c = (ca[:, None, :] * cb[None, :, :] - sa[:, None, :] * sb[None, :, :]).reshape(rows, cols)
    sn = (sa[:, None, :] * cb[None, :, :] + ca[:, None, :] * sb[None, :, :]).reshape(rows, cols)
    return k, s, c, sn


def _angle_tables_scaled(rows, cols, period, mult):
    k = lax.broadcasted_iota(jnp.int32, (rows, cols), 0) * mult
    s = lax.broadcasted_iota(jnp.int32, (rows, cols), 1)
    ang = ((k * s) % period).astype(F32) * (2.0 * math.pi / period)
    return k, s, jnp.cos(ang), jnp.sin(ang)


def hyena_tables(s):
    k, t, c, sn = _angle_tables(s, s, 2 * s)
    alt = lambda idx: (1 - 2 * (idx % 2)).astype(F32)
    fwd = jnp.stack([c, jnp.where(k == 0, alt(t), sn)]).astype(BF16)
    inv = jnp.stack([c, jnp.where(t == 0, alt(k), sn)]).astype(BF16)
    return fwd, inv


def fnet_tables(s):
    _, _, c, sn = _angle_tables(s, s, s)
    return jnp.stack([c, sn]).astype(BF16)


def _seq_dft_body(*refs, nk, epilogue):
    t_ref, x_ref = refs[0], refs[1]
    acc_ref = refs[-1]
    o_ref = refs[-2]
    extra = refs[2:-2]
    kk = pl.program_id(2)

    @pl.when(kk == 0)
    def _():
        acc_ref[...] = jnp.zeros_like(acc_ref)

    xb = x_ref[0].astype(BF16)
    acc_ref[0] += _dot(t_ref[0], xb)
    acc_ref[1] += _dot(t_ref[1], xb)

    @pl.when(kk == nk - 1)
    def _():
        epilogue(acc_ref, extra, o_ref)


def _seq_dft(table, x, wd, col_blk, extra_args, extra_specs, epilogue, out_shape, out_spec, tf, tk, name):
    b, s, _ = x.shape
    tf = min(tf, s)
    tk = min(tk, s)
    nk = s // tk
    return pl.pallas_call(
        functools.partial(_seq_dft_body, nk=nk, epilogue=epilogue),
        grid=(b, s // tf, nk),
        in_specs=[pl.BlockSpec((2, tf, tk), lambda b_, f, kk: (0, f, kk)),
                  pl.BlockSpec((1, tk, wd), lambda b_, f, kk: (b_, kk, col_blk))] + extra_specs,
        out_specs=out_spec,
        out_shape=out_shape,
        scratch_shapes=[pltpu.VMEM((2, tf, wd), F32)],
        compiler_params=_cparams("parallel", "parallel", "arbitrary"),
        name=name,
    )(table, x, *extra_args)


def _hy_prep_body(*refs, ns):
    i = pl.program_id(1)
    w_ref, b_ref = refs[9], refs[10]
    vg_ref, x1_ref = refs[11], refs[12]
    outs = []
    for part in range(3):
        x_ref, xp_ref, xn_ref = refs[3 * part:3 * part + 3]
        outs.append(_conv3(x_ref[0], xp_ref[0], xn_ref[0], w_ref[part], i, ns) + b_ref[part])
    x1, x2, v = outs
    vg_ref[0] = v * x2
    x1_ref[0] = x1


def hy_prep(p3, conv_w, conv_b):
    b, s, _ = p3.shape
    ts = min(512, s)
    ns = s // ts
    ng = HY_W // LANES
    base = P_HY // LANES
    in_specs = []
    for part in range(3):
        in_specs += _halo_specs(ts, s, functools.partial(lambda part_, j: base + part_ * ng + j, part))
    in_specs += [pl.BlockSpec((3, 3, LANES), lambda b_, i, j: (0, 0, j)),
                 pl.BlockSpec((3, 1, LANES), lambda b_, i, j: (0, 0, j))]
    out = pl.BlockSpec((1, ts, LANES), lambda b_, i, j: (b_, i, j))
    cw = jnp.transpose(conv_w.astype(F32).reshape(3, 3, HY_W), (1, 0, 2))
    cb = conv_b.astype(F32).reshape(3, 1, HY_W)
    return pl.pallas_call(
        functools.partial(_hy_prep_body, ns=ns),
        grid=(b, ns, ng),
        in_specs=in_specs,
        out_specs=[out, out],
        out_shape=[jax.ShapeDtypeStruct((b, s, HY_W), F32)] * 2,
        compiler_params=_cparams("parallel", "parallel", "parallel"),
        name="hy_prep",
    )(*([p3] * 9), cw, cb)


def _hy_filter_body(z_ref, w1_ref, b1_ref, w2_ref, b2_ref, w3_ref, b3_ref, w4_ref, b4_ref, fr_ref, dl_ref,
                    buf_ref, sum_ref, *, s, tr):
    i = pl.program_id(0)
    z = z_ref[...]
    fr = fr_ref[...]
    hid = jnp.sin(fr * (_dot(z, w1_ref[...], HIGHEST) + b1_ref[...]))
    hid = jnp.sin(fr * (_dot(hid, w2_ref[...], HIGHEST) + b2_ref[...]))
    hid = jnp.sin(fr * (_dot(hid, w3_ref[...], HIGHEST) + b3_ref[...]))
    hh = _dot(hid, w4_ref[...], HIGHEST) + b4_ref[...]
    window = jnp.exp(-z[:, 0:1] * dl_ref[...])
    row = i * tr + lax.broadcasted_iota(jnp.int32, (tr, HY_W), 0)
    val = jnp.where(row < s, hh[:, :HY_W], hh[:, HY_W:]) * window
    val = jnp.where(row == s, 0.0, val)
    buf_ref[...] = val

    @pl.when(i == 0)
    def _():
        sum_ref[...] = jnp.zeros_like(sum_ref)

    sum_ref[...] += jnp.sum(jnp.abs(val), axis=0, keepdims=True)


def hy_filter(s, w1, b1, w2, b2, w3, b3, w4, b4, freq):
    n2 = 2 * s
    slot = jnp.arange(n2)
    pos = jnp.where(slot <= s, jnp.minimum(slot, s - 1), n2 - slot)
    t_all = jnp.linspace(0.0, 1.0, s, dtype=F32)
    ang_all = (2.0 * math.pi / s) * jnp.arange(s, dtype=F32)
    t = t_all[pos][:, None]
    ang = ang_all[pos][:, None]
    f = jnp.linspace(1e-4, HY_BANDS - 1, HY_BANDS, dtype=F32)[None, :]
    z = jnp.concatenate([t, jnp.cos(f * ang), -jnp.sin(f * ang)], axis=-1)
    z = jnp.pad(z, ((0, 0), (0, LANES - HY_EMB)))
    w1p = jnp.pad(w1.astype(F32), ((0, LANES - HY_EMB), (0, 0)))
    max_decay = math.log(1e-2) / 0.3
    min_decay = math.log(1e-2) / 1.5
    deltas = jnp.abs(jnp.linspace(min_decay, max_decay, HY_W, dtype=F32))[None, :]
    tr = min(512, n2)
    full = lambda a: pl.BlockSpec(a.shape, lambda i: (0,) * a.ndim)
    r2 = lambda a: a.astype(F32).reshape(1, -1)
    args = [z, w1p, r2(b1), w2.astype(F32), r2(b2), w3.astype(F32), r2(b3), w4.astype(F32), r2(b4), r2(freq), deltas]
    return pl.pallas_call(
        functools.partial(_hy_filter_body, s=s, tr=tr),
        grid=(n2 // tr,),
        in_specs=[pl.BlockSpec((tr, LANES), lambda i: (i, 0))] + [full(a) for a in args[1:]],
        out_specs=[pl.BlockSpec((tr, HY_W), lambda i: (i, 0)), pl.BlockSpec((1, HY_W), lambda i: (0, 0))],
        out_shape=[jax.ShapeDtypeStruct((n2, HY_W), F32), jax.ShapeDtypeStruct((1, HY_W), F32)],
        compiler_params=_cparams("arbitrary"),
        name="hy_filter",
    )(*args)


def _spec_raw_epilogue(acc_ref, extra, o_ref):
    o_ref[0] = acc_ref[...]


def _hy_spec_combine_body(a_ref, sum_ref, o_ref, *, tf):
    f = pl.program_id(0)
    k = f * tf + lax.broadcasted_iota(jnp.int32, (tf, HY_W), 0)
    sign = (1 - 2 * (k % 2)).astype(F32)
    inv = 1.0 / sum_ref[...]
    first = a_ref[0]
    second = a_ref[1]
    o_ref[0] = (first[0] + sign * second[0]) * inv
    sign1 = jnp.where(k == 0, 1.0, sign)
    o_ref[1] = (first[1] + sign1 * second[1]) * inv


def hy_filter_spectrum(buf, colsum, fwd_table, s):
    halves = buf.reshape(2, s, HY_W)
    tf = min(DFT_ROWS, s)
    raw = _seq_dft(fwd_table, halves, HY_W, 0, [], [], _spec_raw_epilogue,
                   jax.ShapeDtypeStruct((2, 2, s, HY_W), F32),
                   pl.BlockSpec((1, 2, tf, HY_W), lambda b_, f, kk: (b_, 0, f, 0)), tf, 512, "hy_filter_dft")
    tf = min(512, s)
    return pl.pallas_call(
        functools.partial(_hy_spec_combine_body, tf=tf),
        grid=(s // tf,),
        in_specs=[pl.BlockSpec((2, 2, tf, HY_W), lambda f: (0, 0, f, 0)), pl.BlockSpec((1, HY_W), lambda f: (0, 0))],
        out_specs=pl.BlockSpec((2, tf, HY_W), lambda f: (0, f, 0)),
        out_shape=jax.ShapeDtypeStruct((2, s, HY_W), F32),
        compiler_params=_cparams("parallel"),
        name="hy_filter_combine",
    )(raw, colsum)


def _hy_fwd_epilogue(acc_ref, extra, o_ref, *, s, tf):
    h_ref = extra[0]
    f = pl.program_id(1)
    xc, xs = acc_ref[0], acc_ref[1]
    hc, hs = h_ref[0], h_ref[1]
    row0 = (f * tf + lax.broadcasted_iota(jnp.int32, xc.shape, 0)) == 0
    scale = jnp.where(row0, 1.0 / (2 * s), 2.0 / (2 * s))
    yc = jnp.where(row0, xc * hc, xc * hc - xs * hs) * scale
    ys = jnp.where(row0, xs * hs, xc * hs + xs * hc) * scale
    o_ref[0, 0] = yc.astype(o_ref.dtype)
    o_ref[0, 1] = ys.astype(o_ref.dtype)


def _hy_inv_body(t_ref, y_ref, vg_ref, x1_ref, skip_ref, o_ref, acc_ref, *, nk):
    kk = pl.program_id(2)

    @pl.when(kk == 0)
    def _():
        acc_ref[...] = jnp.zeros_like(acc_ref)

    acc_ref[...] += _dot(t_ref[0], y_ref[0, 0]) + _dot(t_ref[1], y_ref[0, 1])

    @pl.when(kk == nk - 1)
    def _():
        o_ref[0] = ((acc_ref[...] + vg_ref[0] * skip_ref[...]) * x1_ref[0]).astype(o_ref.dtype)


def hyena_mixer(p3, conv_w, conv_b, spectrum, skip, fwd_table, inv_table):
    b, s, _ = p3.shape
    vg, x1 = hy_prep(p3, conv_w, conv_b)
    tf = min(DFT_ROWS, s)
    tk = min(512, s)
    yp = _seq_dft(fwd_table, vg, HY_W, 0, [spectrum], [pl.BlockSpec((2, tf, HY_W), lambda b_, f, kk: (0, f, 0))],
                  functools.partial(_hy_fwd_epilogue, s=s, tf=tf),
                  jax.ShapeDtypeStruct((b, 2, s, HY_W), BF16),
                  pl.BlockSpec((1, 2, tf, HY_W), lambda b_, f, kk: (b_, 0, f, 0)), tf, tk, "hy_fwd_dft")
    nk = s // tk
    tile = pl.BlockSpec((1, tf, HY_W), lambda b_, f, kk: (b_, f, 0))
    return pl.pallas_call(
        functools.partial(_hy_inv_body, nk=nk),
        grid=(b, s // tf, nk),
        in_specs=[pl.BlockSpec((2, tf, tk), lambda b_, f, kk: (0, f, kk)),
                  pl.BlockSpec((1, 2, tk, HY_W), lambda b_, f, kk: (b_, 0, kk, 0)),
                  tile, tile, pl.BlockSpec((1, HY_W), lambda b_, f, kk: (0, 0))],
        out_specs=tile,
        out_shape=jax.ShapeDtypeStruct((b, s, HY_W), BF16),
        scratch_shapes=[pltpu.VMEM((tf, HY_W), F32)],
        compiler_params=_cparams("parallel", "parallel", "arbitrary"),
        name="hy_inv_dft",
    )(inv_table, yp, vg, x1, skip.astype(F32).reshape(1, HY_W))


def _fnet_epilogue(acc_ref, extra, o_ref):
    cw_ref = extra[0]
    for g in range(FN_GROUPS):
        cs = slice(g * FN_GC, (g + 1) * FN_GC)
        y = _dot(acc_ref[0, :, cs].astype(BF16), cw_ref[0, g]) + _dot(acc_ref[1, :, cs].astype(BF16), cw_ref[1, g])
        o_ref[0, :, cs] = y.astype(o_ref.dtype)


def _fnet_fold_body(t_ref, w_ref, o_ref, *, scale):
    for g in range(FN_GROUPS):
        o_ref[0, g] = (_dot(t_ref[0], w_ref[g], HIGHEST) * scale).astype(o_ref.dtype)
        o_ref[1, g] = (_dot(t_ref[1], w_ref[g], HIGHEST) * (-scale)).astype(o_ref.dtype)


def fnet_mixer(p3, w, table):
    b, s, _ = p3.shape
    _, _, cc, sc = _angle_tables(FN_GC, FN_GC, FN_GC)
    folded = pl.pallas_call(
        functools.partial(_fnet_fold_body, scale=1.0 / math.sqrt(s * FN_GC)),
        out_shape=jax.ShapeDtypeStruct((2, FN_GROUPS, FN_GC, FN_GC), BF16),
        name="fnet_fold",
    )(jnp.stack([cc, sc]), w.astype(F32))
    tf = min(DFT_ROWS, s)
    return _seq_dft(table, p3, FN_W, P_FN // FN_W, [folded],
                    [pl.BlockSpec((2, FN_GROUPS, FN_GC, FN_GC), lambda b_, f, kk: (0, 0, 0, 0))],
                    _fnet_epilogue, jax.ShapeDtypeStruct((b, s, FN_W), BF16),
                    pl.BlockSpec((1, tf, FN_W), lambda b_, f, kk: (b_, f, 0)), tf, 512, "fnet_dft")


_CAND_GROUPS = [16, 8, 5, 4, 3, 2, 2, 2]
_CAND_ROWS = sum(-(-g // 8) * 8 for g in _CAND_GROUPS) + 8


def _extract_top(cur_list, val_refs, idx_refs, k):
    rows = [lax.broadcasted_iota(jnp.int32, c.shape, 0) for c in cur_list]

    def step(r, curs):
        out = []
        for t, cur in enumerate(curs):
            m = jnp.max(cur, axis=0, keepdims=True)
            first = jnp.min(jnp.where(cur == m, rows[t], cur.shape[0]), axis=0, keepdims=True)
            if val_refs[t] is not None:
                val_refs[t][pl.ds(r, 1), :] = m
            if idx_refs[t] is not None:
                idx_refs[t][pl.ds(r, 1), :] = first
            out.append(jnp.where(rows[t] == first, -jnp.inf, cur))
        return tuple(out)

    return lax.fori_loop(0, k, step, tuple(cur_list))


def _bf16_bits(x):
    bits = pltpu.bitcast(x, jnp.uint32)
    return (bits + jnp.uint32(0x7FFF) + ((bits >> 16) & jnp.uint32(1))) >> 16


def _peer_score_body(h_ref, wq_ref, k1_ref, k2_ref, cnt_ref, e1_ref, rank2_ref, e2_ref, v_ref, idx_ref, sc_ref, c_ref,
                     pair_ref):
    kk = PEER_TOPK
    tt = h_ref.shape[0]
    nlc = tt // LANES
    q = _dot(h_ref[...], wq_ref[...])
    s1 = _dot_nt(k1_ref[...], q[:, :PEER_HALF].astype(BF16))
    s2 = _dot_nt(k2_ref[...], q[:, PEER_HALF:].astype(BF16))
    for lc in range(nlc):
        ls = slice(lc * LANES, (lc + 1) * LANES)
        _extract_top([s1[:, ls], s2[:, ls]], [v_ref.at[0, lc], v_ref.at[1, lc]],
                     [idx_ref.at[0, lc], idx_ref.at[1, lc]], kk)
    rows8 = lax.broadcasted_iota(jnp.int32, (8, LANES), 0)
    cands = []
    for lc in range(nlc):
        v1 = v_ref[0, lc]
        v2 = v_ref[1, lc]
        pieces = []
        for i, g in enumerate(_CAND_GROUPS):
            gp = -(-g // 8) * 8
            piece = v1[i:i + 1, :] + v2[0:gp, :]
            pieces.append(piece if g == gp else jnp.where(rows8 < g, piece, -jnp.inf))
        pieces.append(v1[8:16, :] + v2[0:1, :])
        cands.append(jnp.concatenate(pieces, axis=0))
    finals = _extract_top(cands, [sc_ref.at[lc] for lc in range(nlc)], [None] * nlc, kk)
    rows = lax.broadcasted_iota(jnp.int32, (PEER_NKEYS, LANES), 0)
    for lc in range(nlc):
        ls = slice(lc * LANES, (lc + 1) * LANES)
        taken = jnp.where((finals[lc] == -jnp.inf) & (cands[lc] > -jnp.inf), 1.0, 0.0)
        off = 0
        for i, g in enumerate(_CAND_GROUPS):
            gp = -(-g // 8) * 8
            c_ref[i:i + 1, :] = jnp.sum(taken[off:off + gp], axis=0, keepdims=True)
            off += gp
        c_ref[8:16, :] = taken[off:off + 8]
        sc = sc_ref[lc]
        z = jnp.sum(jnp.exp(sc - sc[0:1, :]), axis=0, keepdims=True)
        cnt = jnp.zeros((PEER_NKEYS, LANES), F32)
        rank2 = jnp.full((PEER_NKEYS, LANES), float(kk), F32)
        for r in range(kk):
            cnt = jnp.where(rows == idx_ref[0, lc, r:r + 1, :], c_ref[r:r + 1, :], cnt)
            rank2 = jnp.where(rows == idx_ref[1, lc, r:r + 1, :], float(r), rank2)
        cnt_bits = _bf16_bits(cnt)
        e1_bits = _bf16_bits(jnp.exp(s1[:, ls] - v_ref[0, lc, 0:1, :]) / z)
        cnt_ref[0, :, ls] = cnt_bits | (cnt_bits << 16)
        e1_ref[0, :, ls] = e1_bits | (e1_bits << 16)
        pair_ref[0] = _bf16_bits(rank2)
        pair_ref[1] = _bf16_bits(jnp.exp(s2[:, ls] - v_ref[1, lc, 0:1, :]))
        even = pl.ds(0, PEER_NKEYS // 2, stride=2)
        odd = pl.ds(1, PEER_NKEYS // 2, stride=2)
        rank2_ref[0, :, ls] = pair_ref[0, even, :] | (pair_ref[0, odd, :] << 16)
        e2_ref[0, :, ls] = pair_ref[1, even, :] | (pair_ref[1, odd, :] << 16)


def peer_scores(h, wq, k1, k2):
    n, d = h.shape
    tt = min(512, n)
    nlc = tt // LANES
    kk = PEER_TOPK
    rows = jax.ShapeDtypeStruct((PEER_HEADS, PEER_NKEYS, n), jnp.uint32)
    tiles = jax.ShapeDtypeStruct((PEER_HEADS, PEER_NKEYS // 2, n), jnp.uint32)
    blk = pl.BlockSpec((1, PEER_NKEYS, tt), lambda i, h: (h, 0, i))
    blk2 = pl.BlockSpec((1, PEER_NKEYS // 2, tt), lambda i, h: (h, 0, i))
    keys = pl.BlockSpec((PEER_NKEYS, PEER_HALF), lambda i, h: (0, 0))
    return pl.pallas_call(
        _peer_score_body,
        grid=(n // tt, PEER_HEADS),
        in_specs=[pl.BlockSpec((tt, d), lambda i, h: (i, 0)),
                  pl.BlockSpec((d, 2 * PEER_HALF), lambda i, h: (0, h)), keys, keys],
        out_specs=[blk, blk, blk2, blk2],
        out_shape=[rows, rows, tiles, tiles],
        scratch_shapes=[pltpu.VMEM((2, nlc, kk, LANES), F32), pltpu.VMEM((2, nlc, kk, LANES), jnp.int32),
                        pltpu.VMEM((nlc, kk, LANES), F32), pltpu.VMEM((kk, LANES), F32),
                        pltpu.VMEM((2, PEER_NKEYS, LANES), jnp.uint32)],
        compiler_params=_cparams("parallel", "parallel"),
        name="peer_scores",
    )(h, wq, k1.astype(BF16), k2.astype(BF16))


BF16_ROWS = 16


def _peer_dense_body(h_ref, u_ref, vt_ref, cnt_ref, e1_ref, rank2_ref, e2_ref, o_ref, acc_ref, gl_ref, ga_ref,
                     *, ne, r):
    e = pl.program_id(1)
    tt = h_ref.shape[0]
    nblk = PEER_NKEYS // BF16_ROWS

    @pl.when(e == 0)
    def _():
        acc_ref[...] = jnp.zeros_like(acc_ref)

    act = _dot_nt(u_ref[...], h_ref[...])
    gl_ref[...] = (0.5 * act * (1.0 + lax.erf(act * (1.0 / math.sqrt(2.0))))).astype(BF16)
    for rr in range(r):
        e1 = e * r + rr
        cnt_rows = [cnt_ref[hd, pl.ds(e1, 1), :] for hd in range(PEER_HEADS)]
        e1_rows = [e1_ref[hd, pl.ds(e1, 1), :] for hd in range(PEER_HEADS)]
        for lc in range(tt // LANES):
            ls = slice(lc * LANES, (lc + 1) * LANES)
            gate = [None] * nblk
            words = BF16_ROWS // 2
            for hd in range(PEER_HEADS):
                cnt16 = pltpu.bitcast(jnp.broadcast_to(cnt_rows[hd][:, ls], (words, LANES)), BF16)
                w16 = pltpu.bitcast(jnp.broadcast_to(e1_rows[hd][:, ls], (words, LANES)), BF16)
                for kb in range(nblk):
                    ks = slice(kb * words, (kb + 1) * words)
                    rank2 = pltpu.bitcast(rank2_ref[hd, ks, ls], BF16)
                    term = jnp.where(rank2 < cnt16, pltpu.bitcast(e2_ref[hd, ks, ls], BF16), 0.0) * w16
                    gate[kb] = term if gate[kb] is None else gate[kb] + term
            for kb in range(nblk):
                es = slice(rr * PEER_NKEYS + kb * BF16_ROWS, rr * PEER_NKEYS + (kb + 1) * BF16_ROWS)
                ga_ref[es, ls] = gl_ref[es, ls] * gate[kb]
    acc_ref[...] += _dot(vt_ref[...], ga_ref[...])

    @pl.when(e == ne - 1)
    def _():
        o_ref[...] = acc_ref[...].T


def peer_dense(h, u_bf, vt_bf, cnt, e1, rank2, e2):
    n, d = h.shape
    ne_total = u_bf.shape[0]
    tt = min(512, n)
    r = 8
    te = r * PEER_NKEYS
    ne = ne_total // te
    tok = pl.BlockSpec((PEER_HEADS, PEER_NKEYS, tt), lambda i, e: (0, 0, i))
    tok2 = pl.BlockSpec((PEER_HEADS, PEER_NKEYS // 2, tt), lambda i, e: (0, 0, i))
    return pl.pallas_call(
        functools.partial(_peer_dense_body, ne=ne, r=r),
        grid=(n // tt, ne),
        in_specs=[pl.BlockSpec((tt, d), lambda i, e: (i, 0)),
                  pl.BlockSpec((te, d), lambda i, e: (e, 0)),
                  pl.BlockSpec((d, te), lambda i, e: (0, e)),
                  tok, tok, tok2, tok2],
        out_specs=pl.BlockSpec((tt, d), lambda i, e: (i, 0)),
        out_shape=jax.ShapeDtypeStruct((n, d), F32),
        scratch_shapes=[pltpu.VMEM((d, tt), F32), pltpu.VMEM((te, tt), BF16), pltpu.VMEM((te, tt), BF16)],
        compiler_params=_cparams("parallel", "arbitrary"),
        name="peer_dense",
    )(h, u_bf, vt_bf, cnt, e1, rank2, e2)


def peer_ffn(x2d, norm_w, wq, k1, k2, u_tab, v_tab):
    h = rmsnorm(x2d, norm_w, BF16)
    cnt, e1, rank2, e2 = peer_scores(h, wq.astype(BF16), k1, k2)
    return peer_dense(h, u_tab.astype(BF16), v_tab.astype(BF16).T, cnt, e1, rank2, e2)


def _reorder_w_in(w):
    off_z, off_ba = 3 * GDN_W, 4 * GDN_W
    off_hy = off_ba + 4 * GDN_HEADS
    off_fn = off_hy + 3 * HY_W
    pad = jnp.zeros((w.shape[0], LANES - 4 * GDN_HEADS), w.dtype)
    return jnp.concatenate([w[:, :off_z], w[:, off_z:off_ba], w[:, off_hy:off_fn], w[:, off_fn:],
                            w[:, off_ba:off_hy], pad], axis=1)


def kernel(x, norm1_w, w_in, gdn_conv_w, gdn_a_log, gdn_dt_bias, gdn_norm_w, hy_conv_w, hy_conv_b, hy_w1, hy_b1, hy_w2, hy_b2, hy_w3, hy_b3, hy_w4, hy_b4, hy_freq, hy_skip, fnet_w, w_out, norm2_w, peer_wq, peer_k1, peer_k2, peer_u, peer_v, final_norm_w):
    b, s, d = x.shape
    n = b * s
    depth = w_in.shape[0]
    hy_fwd, hy_inv = hyena_tables(s)
    fn_tab = fnet_tables(s)
    x2d = x.reshape(n, d)
    delta = None
    for l in range(depth):
        if delta is None:
            h = rmsnorm(x2d, norm1_w[l], BF16)
        else:
            x2d, h = add_rmsnorm(x2d, delta, norm1_w[l], BF16, True)
        p = linear([(h, _reorder_w_in(w_in[l]).astype(BF16))], None, F32, tm=2048, tn=896)
        p3 = p.reshape(b, s, P_COLS)
        y_a = gdn_mixer(p3, gdn_conv_w[l], gdn_a_log[l], gdn_dt_bias[l], gdn_norm_w[l])
        buf, colsum = hy_filter(s, hy_w1[l], hy_b1[l], hy_w2[l], hy_b2[l], hy_w3[l], hy_b3[l], hy_w4[l], hy_b4[l],
                                hy_freq[l])
        spectrum = hy_filter_spectrum(buf, colsum, hy_fwd, s)
        y_b = hyena_mixer(p3, hy_conv_w[l], hy_conv_b[l], spectrum, hy_skip[l], hy_fwd, hy_inv)
        y_c = fnet_mixer(p3, fnet_w[l], fn_tab)
        wo = w_out[l].astype(BF16)
        x2d = linear([(y_a, wo[:GDN_W]), (y_b.reshape(n, HY_W), wo[GDN_W:GDN_W + HY_W]),
                      (y_c.reshape(n, FN_W), wo[GDN_W + HY_W:])], x2d, F32)
        delta = peer_ffn(x2d, norm2_w[l], peer_wq[l], peer_k1[l], peer_k2[l], peer_u[l], peer_v[l])
    return add_rmsnorm(x2d, delta, final_norm_w, F32, False).reshape(b, s, d)
```
